```python
import jax, jax.numpy as jnp
from jax import lax
import numpy as np

D_MODEL = 1024
BATCH = 2
SEQ = 8192
DEPTH = 4

D_MIX = D_MODEL
HEAD_DIM = 64
A_HEADS = 4
A_WIDTH = A_HEADS * HEAD_DIM
A_CHUNK = 128
B_HEADS = 8
B_DK = 64
B_DV = 64
B_WIDTH = B_HEADS * B_DV
B_CONV = 4
B_CHUNK = 64
C_HEADS = 4
C_DK = 32
C_DV = 64
C_KEY = C_HEADS * C_DK
C_WIDTH = C_HEADS * C_DV
C_RANK = 16
C_TAU = 16.0
C_CHUNK = 64
D_FF = 2816
ALPHA = (2.0 * DEPTH) ** 0.25
BETA_INIT = (8.0 * DEPTH) ** -0.25
EPS = 1e-5
SPLITS = (A_WIDTH, A_WIDTH,
          3 * B_WIDTH, B_WIDTH, B_HEADS, B_HEADS,
          C_KEY, C_KEY, C_WIDTH, C_WIDTH, C_RANK)
D_IN = sum(SPLITS)

kernel_name = "hymba_style_sgu_gdn_gla_macaron_deepnorm"


def layer_norm(x, g, b):
    xf = x.astype(jnp.float32)
    mu = jnp.mean(xf, -1, keepdims=True)
    var = jnp.mean(jnp.square(xf - mu), -1, keepdims=True)
    return ((xf - mu) * lax.rsqrt(var + EPS) * g + b).astype(x.dtype)


def rms_norm(x, g):
    xf = x.astype(jnp.float32)
    return (xf * lax.rsqrt(jnp.mean(xf * xf, -1, keepdims=True) + EPS) * g).astype(x.dtype)


def l2_norm(x):
    xf = x.astype(jnp.float32)
    return (xf * lax.rsqrt(jnp.sum(xf * xf, -1, keepdims=True) + 1e-6)).astype(x.dtype)


def swiglu_half(x, w1, w3, w2):
    return 0.5 * ((jax.nn.silu(x @ w1) * (x @ w3)) @ w2)


def to_chunks(x, c):
    b_, t, h, d = x.shape
    return x.reshape(b_, t // c, c, h, d).transpose(0, 3, 1, 2, 4)


def gates_to_chunks(x, c):
    b_, t, h = x.shape
    return x.reshape(b_, t // c, c, h).transpose(0, 3, 1, 2)


def from_chunks(x):
    b_, h, n, c, d = x.shape
    return x.transpose(0, 2, 3, 1, 4).reshape(b_, n * c, h, d)


def causal_short_conv(x, w):
    k = w.shape[0]
    t = x.shape[1]
    xp = jnp.pad(x, ((0, 0), (k - 1, 0), (0, 0)))
    y = w[0] * xp[:, 0:t]
    for i in range(1, k):
        y = y + w[i] * xp[:, i:i + t]
    return y


def spatial_gating(u, v, ln_g, ln_b, w_s, b_s):
    b_, t, _ = v.shape
    n = t // A_CHUNK
    v = layer_norm(v, ln_g, ln_b).reshape(b_, n, A_CHUNK, A_HEADS, HEAD_DIM)
    mask = jnp.tril(jnp.ones((A_CHUNK, A_CHUNK), dtype=bool))
    w = jnp.where(mask, w_s, 0.0)
    z = jnp.einsum('hij,bnjhd->bnihd', w, v) + b_s.T[:, :, None]
    return u * z.reshape(b_, t, A_WIDTH)


def gated_delta_rule(q, k, v, log_g, beta):
    out_dtype = v.dtype
    q, k, v, log_g, beta = (t_.astype(jnp.float32) for t_ in (q, k, v, log_g, beta))
    c = q.shape[3]
    b_, h, _, _, dk = q.shape
    dv = v.shape[-1]
    incl = jnp.tril(jnp.ones((c, c), dtype=bool))
    strict = jnp.tril(jnp.ones((c, c), dtype=bool), -1)
    gam = jnp.cumsum(log_g, axis=-1)
    decay = jnp.exp(jnp.where(incl, gam[..., :, None] - gam[..., None, :], -jnp.inf))
    kk = jnp.einsum('bhnid,bhnjd->bhnij', k, k)
    lower = jnp.where(strict, beta[..., :, None] * kk * decay, 0.0)
    eye = jnp.eye(c, dtype=jnp.float32)
    rhs = jnp.concatenate([beta[..., None] * v, (beta * jnp.exp(gam))[..., None] * k], axis=-1)
    sol = lax.linalg.triangular_solve(eye + lower, rhs, left_side=True, lower=True, unit_diagonal=True)
    u_base, w_state = sol[..., :dv], sol[..., dv:]
    p_intra = jnp.einsum('bhnid,bhnjd->bhnij', q, k) * decay
    q_dec = q * jnp.exp(gam)[..., None]
    k_dec = k * jnp.exp(gam[..., -1:] - gam)[..., None]
    g_tot = jnp.exp(gam[..., -1])

    def step(s, inp):
        u_b, w_c, p_c, qd, kd, gt = inp
        u = u_b - jnp.einsum('bhck,bhkv->bhcv', w_c, s)
        o = jnp.einsum('bhck,bhkv->bhcv', qd, s) + jnp.einsum('bhij,bhjv->bhiv', p_c, u)
        s = gt[..., None, None] * s + jnp.einsum('bhck,bhcv->bhkv', kd, u)
        return s, o

    s0 = jnp.zeros((b_, h, dk, dv), jnp.float32)
    xs = tuple(jnp.moveaxis(t_, 2, 0) for t_ in (u_base, w_state, p_intra, q_dec, k_dec, g_tot))
    _, o = lax.scan(step, s0, xs)
    return jnp.moveaxis(o, 0, 2).astype(out_dtype)


def gated_deltanet(qkv, z, beta_logit, decay_logit, conv_w, a_log, dt_bias, norm_g):
    b_, t, _ = qkv.shape
    qkv = jax.nn.silu(causal_short_conv(qkv, conv_w))
    q, k, v = jnp.split(qkv, 3, axis=-1)
    q = l2_norm(q.reshape(b_, t, B_HEADS, B_DK)) * (B_DK ** -0.5)
    k = l2_norm(k.reshape(b_, t, B_HEADS, B_DK))
    v = v.reshape(b_, t, B_HEADS, B_DV)
    beta = jax.nn.sigmoid(beta_logit.astype(jnp.float32))
    log_g = -jnp.exp(a_log) * jax.nn.softplus(decay_logit.astype(jnp.float32) + dt_bias)
    o = gated_delta_rule(to_chunks(q, B_CHUNK), to_chunks(k, B_CHUNK), to_chunks(v, B_CHUNK),
                         gates_to_chunks(log_g, B_CHUNK), gates_to_chunks(beta, B_CHUNK))
    o = from_chunks(o)
    o = rms_norm(o, norm_g) * jax.nn.silu(z.reshape(b_, t, B_HEADS, B_DV))
    return o.reshape(b_, t, B_WIDTH)


def gla_chunked(q, k, v, log_a):
    out_dtype = v.dtype
    q, k, v, log_a = (t_.astype(jnp.float32) for t_ in (q, k, v, log_a))
    c = q.shape[3]
    b_, h, _, _, dk = q.shape
    dv = v.shape[-1]
    incl = jnp.tril(jnp.ones((c, c), dtype=bool))
    bcum = jnp.cumsum(log_a, axis=3)
    q_dec = q * jnp.exp(bcum)
    p_intra = jnp.where(incl, jnp.einsum('bhnid,bhnjd->bhnij', q_dec, k * jnp.exp(-bcum)), 0.0)
    o_intra = jnp.einsum('bhnij,bhnjv->bhniv', p_intra, v)
    k_dec = k * jnp.exp(bcum[..., -1:, :] - bcum)
    g_tot = jnp.exp(bcum[..., -1, :])

    def step(s, inp):
        o_i, qd, kd, vc, gt = inp
        o = o_i + jnp.einsum('bhck,bhkv->bhcv', qd, s)
        s = gt[..., :, None] * s + jnp.einsum('bhck,bhcv->bhkv', kd, vc)
        return s, o

    s0 = jnp.zeros((b_, h, dk, dv), jnp.float32)
    xs = tuple(jnp.moveaxis(t_, 2, 0) for t_ in (o_intra, q_dec, k_dec, v, g_tot))
    _, o = lax.scan(step, s0, xs)
    return jnp.moveaxis(o, 0, 2).astype(out_dtype)


def gla(q, k, v, r, g_low, gate_up, gate_b, norm_g):
    b_, t, _ = q.shape
    log_a = jax.nn.log_sigmoid((g_low @ gate_up + gate_b).astype(jnp.float32)) / C_TAU
    q = q.reshape(b_, t, C_HEADS, C_DK) * (C_DK ** -0.5)
    k = k.reshape(b_, t, C_HEADS, C_DK)
    v = v.reshape(b_, t, C_HEADS, C_DV)
    log_a = log_a.reshape(b_, t, C_HEADS, C_DK)
    o = gla_chunked(to_chunks(q, C_CHUNK), to_chunks(k, C_CHUNK), to_chunks(v, C_CHUNK),
                    to_chunks(log_a, C_CHUNK))
    o = rms_norm(from_chunks(o), norm_g) * jax.nn.silu(r.reshape(b_, t, C_HEADS, C_DV))
    return o.reshape(b_, t, C_WIDTH)


def setup_inputs(seed: int = 0) -> dict:
    key = jax.random.key(seed)
    ks = jax.random.split(key, 40)
    L = DEPTH
    f32 = jnp.float32

    def nrm(i, shape, scale):
        return jax.random.normal(ks[i], shape, f32) * scale

    def gain(i, shape):
        return 1.0 + 0.1 * jax.random.normal(ks[i], shape, f32)

    a_val = jax.random.uniform(ks[17], (L, B_HEADS), f32, 1.0, 16.0)
    dt = jnp.exp(jax.random.uniform(ks[18], (L, B_HEADS), f32, np.log(1e-3), np.log(1e-1)))
    return {
        "x": jax.random.normal(ks[0], (BATCH, SEQ, D_MODEL), f32),
        "ffn1_w1": nrm(1, (L, D_MODEL, D_FF), D_MODEL ** -0.5),
        "ffn1_w3": nrm(2, (L, D_MODEL, D_FF), D_MODEL ** -0.5),
        "ffn1_w2": nrm(3, (L, D_FF, D_MODEL), BETA_INIT * D_FF ** -0.5),
        "ln1_g": gain(4, (L, D_MODEL)),
        "ln1_b": nrm(5, (L, D_MODEL), 0.02),
        "w_in": nrm(6, (L, D_MODEL, D_IN), D_MODEL ** -0.5),
        "sgu_ln_g": gain(7, (L, A_WIDTH)),
        "sgu_ln_b": nrm(8, (L, A_WIDTH), 0.02),
        "sgu_w": nrm(9, (L, A_HEADS, A_CHUNK, A_CHUNK), 0.5 * A_CHUNK ** -0.5),
        "sgu_b": gain(10, (L, A_HEADS, A_CHUNK)),
        "gdn_conv_w": nrm(11, (L, B_CONV, 3 * B_WIDTH), 0.5),
        "gdn_a_log": jnp.log(a_val),
        "gdn_dt_bias": dt + jnp.log(-jnp.expm1(-dt)),
        "gdn_norm_g": gain(12, (L, B_DV)),
        "gla_gate_up": nrm(13, (L, C_RANK, C_KEY), C_RANK ** -0.5),
        "gla_gate_b": nrm(14, (L, C_KEY), 0.1),
        "gla_norm_g": gain(15, (L, C_DV)),
        "w_out": nrm(16, (L, D_MIX, D_MODEL), BETA_INIT * D_MIX ** -0.5),
        "ln2_g": gain(19, (L, D_MODEL)),
        "ln2_b": nrm(20, (L, D_MODEL), 0.02),
        "ffn2_w1": nrm(21, (L, D_MODEL, D_FF), D_MODEL ** -0.5),
        "ffn2_w3": nrm(22, (L, D_MODEL, D_FF), D_MODEL ** -0.5),
        "ffn2_w2": nrm(23, (L, D_FF, D_MODEL), BETA_INIT * D_FF ** -0.5),
        "ln3_g": gain(24, (L, D_MODEL)),
        "ln3_b": nrm(25, (L, D_MODEL), 0.02),
    }


def reference(x, ffn1_w1, ffn1_w3, ffn1_w2, ln1_g, ln1_b, w_in, sgu_ln_g, sgu_ln_b, sgu_w, sgu_b,
              gdn_conv_w, gdn_a_log, gdn_dt_bias, gdn_norm_g, gla_gate_up, gla_gate_b, gla_norm_g,
              w_out, ln2_g, ln2_b, ffn2_w1, ffn2_w3, ffn2_w2, ln3_g, ln3_b):
    split_at = [int(s) for s in np.cumsum(SPLITS)[:-1]]
    for l in range(DEPTH):
        x = layer_norm(ALPHA * x + swiglu_half(x, ffn1_w1[l], ffn1_w3[l], ffn1_w2[l]), ln1_g[l], ln1_b[l])
        h = x @ w_in[l]
        (a_u, a_v, b_qkv, b_z, b_beta, b_decay,
         c_q, c_k, c_v, c_r, c_g) = jnp.split(h, split_at, axis=-1)
        out_a = spatial_gating(jax.nn.gelu(a_u), jax.nn.gelu(a_v), sgu_ln_g[l], sgu_ln_b[l], sgu_w[l], sgu_b[l])
        out_b = gated_deltanet(b_qkv, b_z, b_beta, b_decay, gdn_conv_w[l], gdn_a_log[l], gdn_dt_bias[l], gdn_norm_g[l])
        out_c = gla(c_q, c_k, c_v, c_r, c_g, gla_gate_up[l], gla_gate_b[l], gla_norm_g[l])
        mix = jnp.concatenate([out_a, out_b, out_c], axis=-1) @ w_out[l]
        x = layer_norm(ALPHA * x + mix, ln2_g[l], ln2_b[l])
        x = layer_norm(ALPHA * x + swiglu_half(x, ffn2_w1[l], ffn2_w3[l], ffn2_w2[l]), ln3_g[l], ln3_b[l])
    return x
```

```python
import functools

import numpy as np
import jax
import jax.numpy as jnp
from jax import lax
from jax.experimental import pallas as pl
from jax.experimental.pallas import tpu as pltpu

F32 = jnp.float32
BF16 = jnp.bfloat16
HIGHEST = lax.Precision.HIGHEST

D_MODEL = 1024
DEPTH = 4
HEAD_DIM = 64
A_HEADS = 4
A_WIDTH = A_HEADS * HEAD_DIM
A_CHUNK = 128
B_HEADS = 8
B_DK = 64
B_WIDTH = B_HEADS * 64
B_CONV = 4
C_HEADS = 4
C_DK = 32
C_DV = 64
C_KEY = C_HEADS * C_DK
C_WIDTH = C_HEADS * C_DV
C_RANK = 16
C_TAU = 16.0
D_FF = 2816
ALPHA = (2.0 * DEPTH) ** 0.25
EPS = 1e-5

LANES = 128
CHUNK = 64
GATE_W = LANES
D_IN_PAD = 2 * A_WIDTH + 4 * B_WIDTH + 2 * C_WIDTH + 2 * C_KEY + GATE_W
VMEM_LIMIT = 56 * 1024 * 1024

COL_A = 0
COL_BQ, COL_BK, COL_BV, COL_BZ = 1, 2, 3, 4
COL_CV, COL_CR = 10, 11
COL_CQ, COL_CK, COL_G = 24, 25, 26


def _dot(a, b, **kw):
    return jnp.dot(a, b, preferred_element_type=F32, **kw)


def _dot_nt(a, b, **kw):
    return lax.dot_general(a, b, (((1,), (1,)), ((), ())), preferred_element_type=F32, **kw)


def _dot_tn(a, b, **kw):
    return lax.dot_general(a, b, (((0,), (0,)), ((), ())), preferred_element_type=F32, **kw)


def _split2(x):
    hi = x.astype(BF16)
    lo = (x - hi.astype(F32)).astype(BF16)
    return jnp.concatenate([hi, lo], axis=-1)


def _iota(shape, dim):
    return lax.broadcasted_iota(jnp.int32, shape, dim)


def _group_ones(width, group):
    r = _iota((2 * width, width), 0)
    c = _iota((2 * width, width), 1)
    r = jnp.where(r >= width, r - width, r)
    return jnp.where((r // group) == (c // group), 1.0, 0.0).astype(BF16)


def _layer_norm(y, g, b):
    mu = jnp.mean(y, axis=-1, keepdims=True)
    yc = y - mu
    var = jnp.mean(yc * yc, axis=-1, keepdims=True)
    return yc * lax.rsqrt(var + EPS) * g + b


def _silu(x):
    return x * jax.nn.sigmoid(x)


def _gelu_tanh(x):
    return 0.5 * x * (1.0 + jnp.tanh(0.7978845608028654 * (x + 0.044715 * (x * x * x))))


def _softplus(x):
    return jnp.maximum(x, 0.0) + jnp.log1p(jnp.exp(-jnp.abs(x)))


FFN_TM = 512
FFN_TF = 256


def _ffn_ln_kernel(x_ref, w1_ref, w3_ref, w2_ref, g_ref, b_ref, o_ref, a_ref):
    x = x_ref[...]
    xb = x.astype(BF16)
    for j in range(D_FF // FFN_TF):
        sl = slice(j * FFN_TF, (j + 1) * FFN_TF)
        h1 = _dot(xb, w1_ref[:, sl])
        h3 = _dot(xb, w3_ref[:, sl])
        a_ref[:, sl] = (_silu(h1) * h3).astype(BF16)
    y = ALPHA * x + 0.5 * _dot(a_ref[...], w2_ref[...])
    o_ref[...] = _layer_norm(y, g_ref[...], b_ref[...])


def _ffn_ln(x, w1, w3, w2, g, b):
    n = x.shape[0]
    tm = min(FFN_TM, n)
    const = lambda i: (0, 0)
    return pl.pallas_call(
        _ffn_ln_kernel,
        grid=(n // tm,),
        in_specs=[
            pl.BlockSpec((tm, D_MODEL), lambda i: (i, 0)),
            pl.BlockSpec((D_MODEL, D_FF), const, pipeline_mode=pl.Buffered(1)),
            pl.BlockSpec((D_MODEL, D_FF), const, pipeline_mode=pl.Buffered(1)),
            pl.BlockSpec((D_FF, D_MODEL), const, pipeline_mode=pl.Buffered(1)),
            pl.BlockSpec((1, D_MODEL), const),
            pl.BlockSpec((1, D_MODEL), const),
        ],
        out_specs=pl.BlockSpec((tm, D_MODEL), lambda i: (i, 0)),
        out_shape=jax.ShapeDtypeStruct((n, D_MODEL), F32),
        scratch_shapes=[pltpu.VMEM((tm, D_FF), BF16)],
        compiler_params=pltpu.CompilerParams(dimension_semantics=("arbitrary",), vmem_limit_bytes=VMEM_LIMIT),
        name="ffn_ln",
    )(x, w1, w3, w2, g, b)


PROJ_TM = 512


def _proj_kernel(x_ref, w_ref, o_ref):
    o_ref[...] = _dot(x_ref[...].astype(BF16), w_ref[...])


def _proj(x, w):
    n = x.shape[0]
    tm = min(PROJ_TM, n)
    return pl.pallas_call(
        _proj_kernel,
        grid=(n // tm,),
        in_specs=[
            pl.BlockSpec((tm, D_MODEL), lambda i: (i, 0)),
            pl.BlockSpec((D_MODEL, D_IN_PAD), lambda i: (0, 0), pipeline_mode=pl.Buffered(1)),
        ],
        out_specs=pl.BlockSpec((tm, D_IN_PAD), lambda i: (i, 0)),
        out_shape=jax.ShapeDtypeStruct((n, D_IN_PAD), F32),
        compiler_params=pltpu.CompilerParams(dimension_semantics=("arbitrary",), vmem_limit_bytes=VMEM_LIMIT),
        name="in_proj",
    )(x, w)


SGU_TB = 512


def _sgu_kernel(h_ref, lng_ref, lnb_ref, w_ref, bias_ref, o_ref):
    tb = h_ref.shape[0]
    u = _gelu_tanh(h_ref[:, 0:A_WIDTH])
    v = _layer_norm(_gelu_tanh(h_ref[:, A_WIDTH:2 * A_WIDTH]), lng_ref[...], lnb_ref[...])
    tril = _iota((A_CHUNK, A_CHUNK), 1) <= _iota((A_CHUNK, A_CHUNK), 0)
    head_of_lane = _iota((A_CHUNK, A_WIDTH), 1) // HEAD_DIM
    wts = [jnp.where(tril, w_ref[h], 0.0).astype(BF16) for h in range(A_HEADS)]
    bias = bias_ref[...]
    for c in range(tb // A_CHUNK):
        rows = slice(c * A_CHUNK, (c + 1) * A_CHUNK)
        vc = v[rows]
        z = bias
        for h in range(A_HEADS):
            z = z + _dot(wts[h], jnp.where(head_of_lane == h, vc, 0.0).astype(BF16))
        o_ref[rows, :] = (u[rows] * z).astype(o_ref.dtype)


def _sgu(h, lng, lnb, w, bias):
    n = h.shape[0]
    tb = min(SGU_TB, n)
    return pl.pallas_call(
        _sgu_kernel,
        grid=(n // tb,),
        in_specs=[
            pl.BlockSpec((tb, 2 * A_WIDTH), lambda i: (i, COL_A)),
            pl.BlockSpec((1, A_WIDTH), lambda i: (0, 0)),
            pl.BlockSpec((1, A_WIDTH), lambda i: (0, 0)),
            pl.BlockSpec((A_HEADS, A_CHUNK, A_CHUNK), lambda i: (0, 0, 0)),
            pl.BlockSpec((A_CHUNK, A_WIDTH), lambda i: (0, 0)),
        ],
        out_specs=pl.BlockSpec((tb, A_WIDTH), lambda i: (i, 0)),
        out_shape=jax.ShapeDtypeStruct((n, A_WIDTH), BF16),
        compiler_params=pltpu.CompilerParams(dimension_semantics=("arbitrary",), vmem_limit_bytes=VMEM_LIMIT),
        name="sgu",
    )(h, lng, lnb, w, bias)


GLA_TB = 256


def _gla_kernel(q_ref, k_ref, v_ref, r_ref, g_ref, gup_ref, gb_ref, ng_ref, o_ref, st_ref, ob_ref):
    tb = q_ref.shape[0]

    @pl.when(pl.program_id(1) == 0)
    def _():
        st_ref[...] = jnp.zeros_like(st_ref)

    logit = _dot(g_ref[...], gup_ref[...], precision=HIGHEST) + gb_ref[...]
    log_a = -_softplus(-logit) * (1.0 / C_TAU)

    tril = (_iota((CHUNK, CHUNK), 1) <= _iota((CHUNK, CHUNK), 0)).astype(F32)
    khead = _iota((CHUNK, C_KEY), 1) // C_DK
    vhead = _iota((CHUNK, C_WIDTH), 1) // C_DV
    causal = (_iota((CHUNK, C_WIDTH), 1) % CHUNK) <= _iota((CHUNK, C_WIDTH), 0)
    st_mask = (_iota((C_WIDTH, C_KEY), 0) // C_DV) == (_iota((C_WIDTH, C_KEY), 1) // C_DK)

    for c in range(tb // CHUNK):
        rows = slice(c * CHUNK, (c + 1) * CHUNK)
        bc = _dot(tril, log_a[rows], precision=HIGHEST)
        btot = bc[CHUNK - 1:CHUNK, :]
        q = q_ref[rows, :]
        k = k_ref[rows, :]
        v = v_ref[rows, :]
        qd = (q * ((C_DK ** -0.5) * jnp.exp(bc))).astype(BF16)
        kn = k * jnp.exp(-bc)
        kd = (k * jnp.exp(btot - bc)).astype(BF16)
        kstack = jnp.concatenate([jnp.where(khead == h, kn, 0.0) for h in range(C_HEADS)], axis=0).astype(BF16)
        p = jnp.where(causal, _dot_nt(qd, kstack), 0.0).astype(BF16)
        vb = v.astype(BF16)
        vbd = jnp.concatenate([jnp.where(vhead == h, vb, jnp.zeros_like(vb)) for h in range(C_HEADS)], axis=0)
        st = st_ref[...]
        ob_ref[rows, :] = _dot(p, vbd) + _dot_nt(qd, st.astype(BF16))
        upd = _dot_tn(vb, kd)
        st_ref[...] = jnp.exp(btot) * st + jnp.where(st_mask, upd, 0.0)

    o = ob_ref[...]
    ss = _dot(_split2(o * o), _group_ones(C_WIDTH, C_DV))
    o_ref[...] = (o * lax.rsqrt(ss * (1.0 / C_DV) + EPS) * ng_ref[...] * _silu(r_ref[...])).astype(o_ref.dtype)


def _gla(h, batch, seq, gup, gb, ng):
    tb = min(GLA_TB, seq)
    nt = seq // tb
    row = lambda b, t: b * nt + t
    const = lambda b, t: (0, 0)
    return pl.pallas_call(
        _gla_kernel,
        grid=(batch, nt),
        in_specs=[
            pl.BlockSpec((tb, C_KEY), lambda b, t: (row(b, t), COL_CQ)),
            pl.BlockSpec((tb, C_KEY), lambda b, t: (row(b, t), COL_CK)),
            pl.BlockSpec((tb, C_WIDTH), lambda b, t: (row(b, t), COL_CV)),
            pl.BlockSpec((tb, C_WIDTH), lambda b, t: (row(b, t), COL_CR)),
            pl.BlockSpec((tb, GATE_W), lambda b, t: (row(b, t), COL_G)),
            pl.BlockSpec((GATE_W, C_KEY), const),
            pl.BlockSpec((1, C_KEY), const),
            pl.BlockSpec((1, C_WIDTH), const),
        ],
        out_specs=pl.BlockSpec((tb, C_WIDTH), lambda b, t: (row(b, t), 0)),
        out_shape=jax.ShapeDtypeStruct((batch * seq, C_WIDTH), BF16),
        scratch_shapes=[pltpu.VMEM((C_WIDTH, C_KEY), F32), pltpu.VMEM((tb, C_WIDTH), F32)],
        compiler_params=pltpu.CompilerParams(dimension_semantics=("arbitrary", "arbitrary"),
                                             vmem_limit_bytes=VMEM_LIMIT),
        name="gla",
    )(h, h, h, h, h, gup, gb, ng)


GDN_TB = 256
N_PAIRS = B_HEADS // 2
TAIL = 8


def _gdn_kernel(q_ref, k_ref, v_ref, z_ref, g_ref, cw_ref, arow_ref, dtrow_ref, ebg_ref, ng_ref, o_ref,
                xbuf, qn_s, kn_s, v_s, bg_s, rows_s, o_s, s_ref):
    tb = q_ref.shape[0]
    nc = tb // CHUNK
    w3 = 3 * B_WIDTH

    @pl.when(pl.program_id(1) == 0)
    def _():
        s_ref[...] = jnp.zeros_like(s_ref)
        xbuf[0:TAIL, :] = jnp.zeros((TAIL, w3), F32)

    xbuf[TAIL:TAIL + tb, 0:B_WIDTH] = q_ref[...]
    xbuf[TAIL:TAIL + tb, B_WIDTH:2 * B_WIDTH] = k_ref[...]
    xbuf[TAIL:TAIL + tb, 2 * B_WIDTH:w3] = v_ref[...]
    y = cw_ref[B_CONV - 1:B_CONV, :] * xbuf[TAIL:TAIL + tb, :]
    for i in range(B_CONV - 1):
        back = B_CONV - 1 - i
        y = y + cw_ref[i:i + 1, :] * xbuf[TAIL - back:TAIL - back + tb, :]
    tail = xbuf[tb:tb + TAIL, :]
    xbuf[0:TAIL, :] = tail
    y = _silu(y)

    ones2 = _group_ones(LANES, B_DK)
    for p in range(N_PAIRS):
        sl = slice(p * LANES, (p + 1) * LANES)
        yq = y[:, sl]
        qn_s[:, sl] = yq * (lax.rsqrt(_dot(_split2(yq * yq), ones2) + 1e-6) * (B_DK ** -0.5))
        yk = y[:, B_WIDTH + p * LANES:B_WIDTH + (p + 1) * LANES]
        kn_s[:, sl] = yk * lax.rsqrt(_dot(_split2(yk * yk), ones2) + 1e-6)
    v_s[...] = y[:, 2 * B_WIDTH:w3]

    lane = _iota((tb, GATE_W), 1)
    gates = g_ref[...]
    beta = jax.nn.sigmoid(gates)
    lg = jnp.where((lane >= B_HEADS) & (lane < 2 * B_HEADS),
                   -jnp.exp(arow_ref[...]) * _softplus(gates + dtrow_ref[...]), 0.0)
    ri = _iota((tb, tb), 0)
    ci = _iota((tb, tb), 1)
    same_chunk = (ri // CHUNK) == (ci // CHUNK)
    blk_tril = jnp.where(same_chunk & (ci <= ri), 1.0, 0.0)
    gam = _dot(blk_tril, lg, precision=HIGHEST)
    feat = jnp.where(lane < B_HEADS, beta, gam)
    bg_s[...] = _dot(_split2(feat), ebg_ref[...])

    lg_t = lg.T[0:2 * B_HEADS, :]
    blk_triu = jnp.where(same_chunk & (ri <= ci), 1.0, 0.0)
    gam_t = _dot(lg_t, blk_triu, precision=HIGHEST)
    half_swapped = (ci % LANES) < CHUNK
    ci_sw = jnp.where(half_swapped, ci + CHUNK, ci - CHUNK)
    blk_triu_sw = jnp.where(((ri // CHUNK) == (ci_sw // CHUNK)) & (ri <= ci_sw), 1.0, 0.0)
    gam_t_sw = _dot(lg_t, blk_triu_sw, precision=HIGHEST)
    low_half = _iota((1, LANES), 1) < CHUNK
    for c in range(nc):
        tile = slice((c // 2) * LANES, (c // 2 + 1) * LANES)
        first, second = (gam_t, gam_t_sw) if c % 2 == 0 else (gam_t_sw, gam_t)
        for p in range(N_PAIRS):
            h0 = B_HEADS + 2 * p
            rows_s[c, p:p + 1, :] = jnp.where(low_half, first[h0:h0 + 1, tile], second[h0 + 1:h0 + 2, tile])

    ii = _iota((CHUNK, LANES), 0)
    jj = _iota((CHUNK, LANES), 1) % CHUNK
    incl = jj <= ii
    strict = jj < ii
    eye = jnp.where(jj == ii, 1.0, 0.0)
    lo_lane = _iota((CHUNK, LANES), 1) < CHUNK
    bd_mask = (_iota((LANES, LANES), 0) // CHUNK) == (_iota((LANES, LANES), 1) // CHUNK)

    def blockdiag(m):
        zero = jnp.zeros_like(m)
        return jnp.concatenate([jnp.where(lo_lane, m, zero), jnp.where(lo_lane, zero, m)], axis=0)

    def chunk_step(c, carry):
        r0 = pl.multiple_of(c * CHUNK, CHUNK)
        rws = pl.ds(r0, CHUNK)
        grows = rows_s[c]
        for p in range(N_PAIRS):
            sl = slice(p * LANES, (p + 1) * LANES)
            q = qn_s[rws, sl]
            k = kn_s[rws, sl]
            v = v_s[rws, sl]
            bet = bg_s[rws, sl]
            gi = bg_s[rws, B_WIDTH + p * LANES:B_WIDTH + (p + 1) * LANES]
            gj = grows[p:p + 1, :]
            dec = jnp.exp(jnp.minimum(gi - gj, 0.0))
            eg = jnp.exp(gi)
            glast = gi[CHUNK - 1:CHUNK, :]
            kb16 = k.astype(BF16)
            qk = _dot_nt(jnp.concatenate([q.astype(BF16), kb16], axis=0), blockdiag(kb16))
            pm = jnp.where(incl, qk[0:CHUNK] * dec, 0.0)
            nk = jnp.where(strict, qk[CHUNK:2 * CHUNK] * (-bet) * dec, 0.0)
            tm = eye + nk
            nk = _dot(nk.astype(BF16), blockdiag(nk.astype(BF16)))
            for _ in range(4):
                n16 = nk.astype(BF16)
                both = _dot(n16, jnp.concatenate([blockdiag(n16), blockdiag(tm.astype(BF16))], axis=1))
                nk = both[:, 0:LANES]
                tm = tm + both[:, LANES:2 * LANES]
            tm = tm + _dot(nk.astype(BF16), blockdiag(tm.astype(BF16)))
            rhs = jnp.concatenate([blockdiag((v * bet).astype(BF16)), blockdiag((k * (bet * eg)).astype(BF16))], axis=1)
            sol = _dot(tm.astype(BF16), rhs)
            u_base = sol[:, 0:LANES]
            w_state = sol[:, LANES:2 * LANES]
            s = s_ref[p]
            s16 = s.astype(BF16)
            ws = _dot(jnp.concatenate([w_state.astype(BF16), (q * eg).astype(BF16)], axis=0), s16)
            u = u_base - ws[0:CHUNK]
            u16 = u.astype(BF16)
            o_s[rws, sl] = ws[CHUNK:2 * CHUNK] + _dot(pm.astype(BF16), blockdiag(u16))
            kd = (k * jnp.exp(glast - gi)).astype(BF16)
            s_ref[p] = jnp.exp(glast) * s + jnp.where(bd_mask, _dot_tn(kd, u16), 0.0)
        return carry

    lax.fori_loop(0, nc, chunk_step, 0)

    for p in range(N_PAIRS):
        sl = slice(p * LANES, (p + 1) * LANES)
        o = o_s[:, sl]
        ss = _dot(_split2(o * o), ones2)
        o_ref[:, sl] = (o * lax.rsqrt(ss * (1.0 / 64.0) + EPS) * ng_ref[...] * _silu(z_ref[:, sl])).astype(o_ref.dtype)


def _gdn(h, batch, seq, cw, arow, dtrow, ebg, ng):
    tb = min(GDN_TB, seq)
    nt = seq // tb
    row = lambda b, t: b * nt + t
    const = lambda b, t: (0, 0)
    return pl.pallas_call(
        _gdn_kernel,
        grid=(batch, nt),
        in_specs=[
            pl.BlockSpec((tb, B_WIDTH), lambda b, t: (row(b, t), COL_BQ)),
            pl.BlockSpec((tb, B_WIDTH), lambda b, t: (row(b, t), COL_BK)),
            pl.BlockSpec((tb, B_WIDTH), lambda b, t: (row(b, t), COL_BV)),
            pl.BlockSpec((tb, B_WIDTH), lambda b, t: (row(b, t), COL_BZ)),
            pl.BlockSpec((tb, GATE_W), lambda b, t: (row(b, t), COL_G)),
            pl.BlockSpec((B_CONV, 3 * B_WIDTH), const),
            pl.BlockSpec((1, GATE_W), const),
            pl.BlockSpec((1, GATE_W), const),
            pl.BlockSpec((2 * GATE_W, 2 * B_WIDTH), const),
            pl.BlockSpec((1, LANES), const),
        ],
        out_specs=pl.BlockSpec((tb, B_WIDTH), lambda b, t: (row(b, t), 0)),
        out_shape=jax.ShapeDtypeStruct((batch * seq, B_WIDTH), BF16),
        scratch_shapes=[
            pltpu.VMEM((tb + TAIL, 3 * B_WIDTH), F32),
            pltpu.VMEM((tb, B_WIDTH), F32),
            pltpu.VMEM((tb, B_WIDTH), F32),
            pltpu.VMEM((tb, B_WIDTH), F32),
            pltpu.VMEM((tb, 2 * B_WIDTH), F32),
            pltpu.VMEM((tb // CHUNK, 8, LANES), F32),
            pltpu.VMEM((tb, B_WIDTH), F32),
            pltpu.VMEM((N_PAIRS, LANES, LANES), F32),
        ],
        compiler_params=pltpu.CompilerParams(dimension_semantics=("arbitrary", "arbitrary"),
                                             vmem_limit_bytes=VMEM_LIMIT),
        name="gdn",
    )(h, h, h, h, h, cw, arow, dtrow, ebg, ng)


OUT_TM = 512


def _out_ln_kernel(x_ref, a_ref, b_ref, c_ref, w_ref, g_ref, bb_ref, o_ref):
    mix = _dot(a_ref[...], w_ref[0:A_WIDTH, :])
    mix = mix + _dot(b_ref[...], w_ref[A_WIDTH:A_WIDTH + B_WIDTH, :])
    mix = mix + _dot(c_ref[...], w_ref[A_WIDTH + B_WIDTH:, :])
    o_ref[...] = _layer_norm(ALPHA * x_ref[...] + mix, g_ref[...], bb_ref[...])


def _out_ln(x, a, b, c, w, g, bb):
    n = x.shape[0]
    tm = min(OUT_TM, n)
    const = lambda i: (0, 0)
    return pl.pallas_call(
        _out_ln_kernel,
        grid=(n // tm,),
        in_specs=[
            pl.BlockSpec((tm, D_MODEL), lambda i: (i, 0)),
            pl.BlockSpec((tm, A_WIDTH), lambda i: (i, 0)),
            pl.BlockSpec((tm, B_WIDTH), lambda i: (i, 0)),
            pl.BlockSpec((tm, C_WIDTH), lambda i: (i, 0)),
            pl.BlockSpec((D_MODEL, D_MODEL), const),
            pl.BlockSpec((1, D_MODEL), const),
            pl.BlockSpec((1, D_MODEL), const),
        ],
        out_specs=pl.BlockSpec((tm, D_MODEL), lambda i: (i, 0)),
        out_shape=jax.ShapeDtypeStruct((n, D_MODEL), F32),
        compiler_params=pltpu.CompilerParams(dimension_semantics=("arbitrary",), vmem_limit_bytes=VMEM_LIMIT),
        name="out_ln",
    )(x, a, b, c, w, g, bb)


def _pad_w_in(w_in):
    o = np.cumsum([0, A_WIDTH, A_WIDTH, 3 * B_WIDTH, B_WIDTH, B_HEADS, B_HEADS, C_KEY, C_KEY, C_WIDTH, C_WIDTH, C_RANK])
    a_u, a_v, b_qkv, b_z, b_beta, b_dec, c_q, c_k, c_v, c_r, c_g = (w_in[..., o[i]:o[i + 1]] for i in range(11))
    pad = jnp.zeros(w_in.shape[:-1] + (GATE_W - 2 * B_HEADS - C_RANK,), w_in.dtype)
    return jnp.concatenate([a_u, a_v, b_qkv, b_z, c_v, c_r, c_q, c_k, b_beta, b_dec, c_g, pad], axis=-1).astype(BF16)


def _bcast_selector():
    e = np.zeros((2 * GATE_W, 2 * B_WIDTH), np.float32)
    for j in range(B_WIDTH):
        hd = j // 64
        for base in (0, GATE_W):
            e[base + hd, j] = 1.0
            e[base + B_HEADS + hd, B_WIDTH + j] = 1.0
    return jnp.asarray(e, BF16)


def _gate_row(vec, offset):
    l, n = vec.shape
    return jnp.zeros((l, 1, GATE_W), F32).at[:, 0, offset:offset + n].set(vec)


def kernel(x, ffn1_w1, ffn1_w3, ffn1_w2, ln1_g, ln1_b, w_in, sgu_ln_g, sgu_ln_b, sgu_w, sgu_b, gdn_conv_w, gdn_a_log, gdn_dt_bias, gdn_norm_g, gla_gate_up, gla_gate_b, gla_norm_g, w_out, ln2_g, ln2_b, ffn2_w1, ffn2_w3, ffn2_w2, ln3_g, ln3_b):
    batch, seq, d = x.shape
    depth = w_in.shape[0]
    row = lambda p: p[:, None, :]
    f1 = [w.astype(BF16) for w in (ffn1_w1, ffn1_w3, ffn1_w2)]
    f2 = [w.astype(BF16) for w in (ffn2_w1, ffn2_w3, ffn2_w2)]
    w_in_p = _pad_w_in(w_in)
    w_out_b = w_out.astype(BF16)
    sgu_bias = jnp.repeat(jnp.swapaxes(sgu_b, 1, 2), HEAD_DIM, axis=2)
    arow = _gate_row(gdn_a_log, B_HEADS)
    dtrow = _gate_row(gdn_dt_bias, B_HEADS)
    ebg = _bcast_selector()
    gdn_ng = jnp.tile(gdn_norm_g, (1, 2))[:, None, :]
    gla_ng = jnp.tile(gla_norm_g, (1, C_HEADS))[:, None, :]
    gup = jnp.zeros((depth, GATE_W, C_KEY), F32).at[:, 2 * B_HEADS:2 * B_HEADS + C_RANK, :].set(gla_gate_up)

    xf = x.reshape(batch * seq, d)
    for l in range(depth):
        xf = _ffn_ln(xf, f1[0][l], f1[1][l], f1[2][l], row(ln1_g)[l], row(ln1_b)[l])
        h = _proj(xf, w_in_p[l])
        out_a = _sgu(h, row(sgu_ln_g)[l], row(sgu_ln_b)[l], sgu_w[l], sgu_bias[l])
        out_b = _gdn(h, batch, seq, gdn_conv_w[l], arow[l], dtrow[l], ebg, gdn_ng[l])
        out_c = _gla(h, batch, seq, gup[l], row(gla_gate_b)[l], gla_ng[l])
        xf = _out_ln(xf, out_a, out_b, out_c, w_out_b[l], row(ln2_g)[l], row(ln2_b)[l])
        xf = _ffn_ln(xf, f2[0][l], f2[1][l], f2[2][l], row(ln3_g)[l], row(ln3_b)[l])
    return xf.reshape(batch, seq, d)
```

```python
import numpy as np
import jax
import jax.numpy as jnp
from jax import lax
from jax.experimental import pallas as pl
from jax.experimental.pallas import tpu as pltpu

F32 = jnp.float32
BF16 = jnp.bfloat16
HIGHEST = lax.Precision.HIGHEST

D_MODEL = 1024
DEPTH = 4
HEAD_DIM = 64
A_HEADS = 4
A_WIDTH = A_HEADS * HEAD_DIM
A_CHUNK = 128
B_HEADS = 8
B_DK = 64
B_WIDTH = B_HEADS * 64
B_CONV = 4
C_HEADS = 4
C_DK = 32
C_DV = 64
C_KEY = C_HEADS * C_DK
C_WIDTH = C_HEADS * C_DV
C_RANK = 16
C_TAU = 16.0
D_FF = 2816
ALPHA = (2.0 * DEPTH) ** 0.25
EPS = 1e-5

LANES = 128
CHUNK = 64
GATE_W = LANES
D_IN_PAD = 2 * A_WIDTH + 4 * B_WIDTH + 2 * C_WIDTH + 2 * C_KEY + GATE_W
VMEM_LIMIT = 56 * 1024 * 1024

COL_A = 0
COL_BQ, COL_BK, COL_BV, COL_BZ = 1, 2, 3, 4
COL_CV, COL_CR = 10, 11
COL_CQ, COL_CK, COL_G = 24, 25, 26


def _dot(a, b, **kw):
    return jnp.dot(a, b, preferred_element_type=F32, **kw)


def _dot_nt(a, b, **kw):
    return lax.dot_general(a, b, (((1,), (1,)), ((), ())), preferred_element_type=F32, **kw)


def _dot_tn(a, b, **kw):
    return lax.dot_general(a, b, (((0,), (0,)), ((), ())), preferred_element_type=F32, **kw)


def _split2(x):
    hi = x.astype(BF16)
    lo = (x - hi.astype(F32)).astype(BF16)
    return jnp.concatenate([hi, lo], axis=-1)


def _split3(x, axis):
    hi = x.astype(BF16)
    r1 = x - hi.astype(F32)
    mid = r1.astype(BF16)
    lo = (r1 - mid.astype(F32)).astype(BF16)
    return jnp.concatenate([hi, mid, lo], axis=axis)


def _iota(shape, dim):
    return lax.broadcasted_iota(jnp.int32, shape, dim)


def _group_ones(width, group):
    r = _iota((width, width), 0)
    c = _iota((width, width), 1)
    return jnp.where((r // group) == (c // group), 1.0, 0.0).astype(BF16)


def _group_sumsq(x, ones):
    return _dot((x * x).astype(BF16), ones)


def _layer_norm(y, g, b):
    mu = jnp.mean(y, axis=-1, keepdims=True)
    yc = y - mu
    var = jnp.mean(yc * yc, axis=-1, keepdims=True)
    return yc * lax.rsqrt(var + EPS) * g + b


def _silu(x):
    return x * jax.nn.sigmoid(x)


def _gelu_tanh(x):
    return 0.5 * x * (1.0 + jnp.tanh(0.7978845608028654 * (x + 0.044715 * (x * x * x))))


def _softplus(x):
    return jnp.maximum(x, 0.0) + jnp.log(1.0 + jnp.exp(-jnp.abs(x)))


FFN_TM = 512
FFN_TF = 256


def _ffn_ln_kernel(x_ref, w1_ref, w3_ref, w2_ref, g_ref, b_ref, o_ref, a_ref):
    x = x_ref[...]
    xb = x.astype(BF16)
    for j in range(D_FF // FFN_TF):
        sl = slice(j * FFN_TF, (j + 1) * FFN_TF)
        h1 = _dot(xb, w1_ref[:, sl])
        h3 = _dot(xb, w3_ref[:, sl])
        a_ref[:, sl] = (_silu(h1) * h3).astype(BF16)
    y = ALPHA * x + 0.5 * _dot(a_ref[...], w2_ref[...])
    o_ref[...] = _layer_norm(y, g_ref[...], b_ref[...])


def _layer_spec(shape, l, **kw):
    zeros = (0,) * len(shape)
    return pl.BlockSpec((None,) + tuple(shape), lambda *_: (l,) + zeros, **kw)


def _ffn_ln(x, w1, w3, w2, g, b, l):
    n = x.shape[0]
    tm = min(FFN_TM, n)
    return pl.pallas_call(
        _ffn_ln_kernel,
        grid=(n // tm,),
        in_specs=[
            pl.BlockSpec((tm, D_MODEL), lambda i: (i, 0)),
            _layer_spec((D_MODEL, D_FF), l, pipeline_mode=pl.Buffered(1)),
            _layer_spec((D_MODEL, D_FF), l, pipeline_mode=pl.Buffered(1)),
            _layer_spec((D_FF, D_MODEL), l, pipeline_mode=pl.Buffered(1)),
            _layer_spec((1, D_MODEL), l),
            _layer_spec((1, D_MODEL), l),
        ],
        out_specs=pl.BlockSpec((tm, D_MODEL), lambda i: (i, 0)),
        out_shape=jax.ShapeDtypeStruct((n, D_MODEL), F32),
        scratch_shapes=[pltpu.VMEM((tm, D_FF), BF16)],
        compiler_params=pltpu.CompilerParams(dimension_semantics=("arbitrary",), vmem_limit_bytes=VMEM_LIMIT),
        name="ffn_ln",
    )(x, w1, w3, w2, g, b)


PROJ_TM = 512


def _proj_kernel(x_ref, w_ref, o_ref):
    o_ref[...] = _dot(x_ref[...].astype(BF16), w_ref[...])


def _proj(x, w, l):
    n = x.shape[0]
    tm = min(PROJ_TM, n)
    return pl.pallas_call(
        _proj_kernel,
        grid=(n // tm,),
        in_specs=[
            pl.BlockSpec((tm, D_MODEL), lambda i: (i, 0)),
            _layer_spec((D_MODEL, D_IN_PAD), l, pipeline_mode=pl.Buffered(1)),
        ],
        out_specs=pl.BlockSpec((tm, D_IN_PAD), lambda i: (i, 0)),
        out_shape=jax.ShapeDtypeStruct((n, D_IN_PAD), F32),
        compiler_params=pltpu.CompilerParams(dimension_semantics=("arbitrary",), vmem_limit_bytes=VMEM_LIMIT),
        name="in_proj",
    )(x, w)


SGU_TB = 512


def _sgu_kernel(h_ref, lng_ref, lnb_ref, w_ref, bias_ref, o_ref):
    tb = h_ref.shape[0]
    u = _gelu_tanh(h_ref[:, 0:A_WIDTH])
    v = _layer_norm(_gelu_tanh(h_ref[:, A_WIDTH:2 * A_WIDTH]), lng_ref[...], lnb_ref[...])
    tril = _iota((A_CHUNK, A_CHUNK), 1) <= _iota((A_CHUNK, A_CHUNK), 0)
    head_of_lane = _iota((A_CHUNK, A_WIDTH), 1) // HEAD_DIM
    wts = [jnp.where(tril, w_ref[h], 0.0).astype(BF16) for h in range(A_HEADS)]
    bias = bias_ref[...]
    for c in range(tb // A_CHUNK):
        rows = slice(c * A_CHUNK, (c + 1) * A_CHUNK)
        vc = v[rows]
        z = bias
        for h in range(A_HEADS):
            z = z + _dot(wts[h], jnp.where(head_of_lane == h, vc, 0.0).astype(BF16))
        o_ref[rows, :] = (u[rows] * z).astype(o_ref.dtype)


def _sgu(h, lng, lnb, w, bias, l):
    n = h.shape[0]
    tb = min(SGU_TB, n)
    return pl.pallas_call(
        _sgu_kernel,
        grid=(n // tb,),
        in_specs=[
            pl.BlockSpec((tb, 2 * A_WIDTH), lambda i: (i, COL_A)),
            _layer_spec((1, A_WIDTH), l),
            _layer_spec((1, A_WIDTH), l),
            _layer_spec((A_HEADS, A_CHUNK, A_CHUNK), l),
            _layer_spec((A_CHUNK, A_WIDTH), l),
        ],
        out_specs=pl.BlockSpec((tb, A_WIDTH), lambda i: (i, 0)),
        out_shape=jax.ShapeDtypeStruct((n, A_WIDTH), BF16),
        compiler_params=pltpu.CompilerParams(dimension_semantics=("arbitrary",), vmem_limit_bytes=VMEM_LIMIT),
        name="sgu",
    )(h, lng, lnb, w, bias)


GLA_TB = 256


def _gla_kernel(q_ref, k_ref, v_ref, r_ref, g_ref, gup_ref, gb_ref, ng_ref, o_ref, st_ref, ob_ref):
    tb = q_ref.shape[0]

    @pl.when(pl.program_id(1) == 0)
    def _():
        st_ref[...] = jnp.zeros_like(st_ref)

    logit = _dot(g_ref[...], gup_ref[...], precision=HIGHEST) + gb_ref[...]
    log_a = -_softplus(-logit) * (1.0 / C_TAU)

    tril = (_iota((CHUNK, CHUNK), 1) <= _iota((CHUNK, CHUNK), 0)).astype(F32)
    khead = _iota((CHUNK, C_KEY), 1) // C_DK
    vhead = _iota((CHUNK, C_WIDTH), 1) // C_DV
    causal = (_iota((CHUNK, C_WIDTH), 1) % CHUNK) <= _iota((CHUNK, C_WIDTH), 0)
    st_mask = (_iota((C_WIDTH, C_KEY), 0) // C_DV) == (_iota((C_WIDTH, C_KEY), 1) // C_DK)

    for c in range(tb // CHUNK):
        rows = slice(c * CHUNK, (c + 1) * CHUNK)
        bc = _dot(tril, log_a[rows], precision=HIGHEST)
        btot = bc[CHUNK - 1:CHUNK, :]
        q = q_ref[rows, :]
        k = k_ref[rows, :]
        v = v_ref[rows, :]
        qd = (q * ((C_DK ** -0.5) * jnp.exp(bc))).astype(BF16)
        kn = k * jnp.exp(-bc)
        kd = (k * jnp.exp(btot - bc)).astype(BF16)
        kstack = jnp.concatenate([jnp.where(khead == h, kn, 0.0) for h in range(C_HEADS)], axis=0).astype(BF16)
        p = jnp.where(causal, _dot_nt(qd, kstack), 0.0).astype(BF16)
        vb = v.astype(BF16)
        vbd = jnp.concatenate([jnp.where(vhead == h, vb, jnp.zeros_like(vb)) for h in range(C_HEADS)], axis=0)
        st = st_ref[...]
        ob_ref[rows, :] = _dot(p, vbd) + _dot_nt(qd, st.astype(BF16))
        upd = _dot_tn(vb, kd)
        st_ref[...] = jnp.exp(btot) * st + jnp.where(st_mask, upd, 0.0)

    o = ob_ref[...]
    ss = _group_sumsq(o, _group_ones(C_WIDTH, C_DV))
    o_ref[...] = (o * lax.rsqrt(ss * (1.0 / C_DV) + EPS) * ng_ref[...] * _silu(r_ref[...])).astype(o_ref.dtype)


def _gla(h, batch, seq, gup, gb, ng, l):
    tb = min(GLA_TB, seq)
    nt = seq // tb
    row = lambda b, t: b * nt + t
    return pl.pallas_call(
        _gla_kernel,
        grid=(batch, nt),
        in_specs=[
            pl.BlockSpec((tb, C_KEY), lambda b, t: (row(b, t), COL_CQ)),
            pl.BlockSpec((tb, C_KEY), lambda b, t: (row(b, t), COL_CK)),
            pl.BlockSpec((tb, C_WIDTH), lambda b, t: (row(b, t), COL_CV)),
            pl.BlockSpec((tb, C_WIDTH), lambda b, t: (row(b, t), COL_CR)),
            pl.BlockSpec((tb, GATE_W), lambda b, t: (row(b, t), COL_G)),
            _layer_spec((GATE_W, C_KEY), l),
            _layer_spec((1, C_KEY), l),
            _layer_spec((1, C_WIDTH), l),
        ],
        out_specs=pl.BlockSpec((tb, C_WIDTH), lambda b, t: (row(b, t), 0)),
        out_shape=jax.ShapeDtypeStruct((batch * seq, C_WIDTH), BF16),
        scratch_shapes=[pltpu.VMEM((C_WIDTH, C_KEY), F32), pltpu.VMEM((tb, C_WIDTH), F32)],
        compiler_params=pltpu.CompilerParams(dimension_semantics=("arbitrary", "arbitrary"),
                                             vmem_limit_bytes=VMEM_LIMIT),
        name="gla",
    )(h, h, h, h, h, gup, gb, ng)


GDN_TB = 256
N_PAIRS = B_HEADS // 2
TAIL = 8
GROUP_CHUNKS = 4


def _gdn_kernel(q_ref, k_ref, v_ref, z_ref, g_ref, cw_ref, arow_ref, dtrow_ref, ebg_ref, tri_ref, ng_ref, o_ref,
                xbuf, qn_s, kn_s, v_s, bg_s, rows_s, qa_s, bb_s, gt_s, o_s, s_ref):
    nb, tb = q_ref.shape[0], q_ref.shape[1]
    nc = tb // CHUNK
    w3 = 3 * B_WIDTH

    @pl.when(pl.program_id(0) == 0)
    def _():
        s_ref[...] = jnp.zeros_like(s_ref)
        xbuf[:, 0:TAIL, :] = jnp.zeros((nb, TAIL, w3), F32)

    ones2 = _group_ones(LANES, B_DK)
    lane = _iota((tb, GATE_W), 1)
    low_half = _iota((1, LANES), 1) < CHUNK
    for b in range(nb):
        xbuf[b, TAIL:TAIL + tb, 0:B_WIDTH] = q_ref[b]
        xbuf[b, TAIL:TAIL + tb, B_WIDTH:2 * B_WIDTH] = k_ref[b]
        xbuf[b, TAIL:TAIL + tb, 2 * B_WIDTH:w3] = v_ref[b]
        y = cw_ref[B_CONV - 1:B_CONV, :] * xbuf[b, TAIL:TAIL + tb, :]
        for i in range(B_CONV - 1):
            back = B_CONV - 1 - i
            y = y + cw_ref[i:i + 1, :] * xbuf[b, TAIL - back:TAIL - back + tb, :]
        xbuf[b, 0:TAIL, :] = xbuf[b, tb:tb + TAIL, :]
        y = _silu(y)
        for p in range(N_PAIRS):
            sl = slice(p * LANES, (p + 1) * LANES)
            yq = y[:, sl]
            qn_s[b, :, sl] = yq * (lax.rsqrt(_group_sumsq(yq, ones2) + 1e-6) * (B_DK ** -0.5))
            yk = y[:, B_WIDTH + p * LANES:B_WIDTH + (p + 1) * LANES]
            kn_s[b, :, sl] = yk * lax.rsqrt(_group_sumsq(yk, ones2) + 1e-6)
        v_s[b] = y[:, 2 * B_WIDTH:w3]

        gates = g_ref[b]
        beta = jax.nn.sigmoid(gates)
        lg = jnp.where((lane >= B_HEADS) & (lane < 2 * B_HEADS),
                       -jnp.exp(arow_ref[...]) * _softplus(gates + dtrow_ref[...]), 0.0)
        g3 = _dot(tri_ref[0], _split3(lg, 1))
        gam = g3[:, 0:GATE_W] + g3[:, GATE_W:2 * GATE_W] + g3[:, 2 * GATE_W:3 * GATE_W]
        feat = jnp.where(lane < B_HEADS, beta, gam)
        bg_s[b] = _dot(_split2(feat), ebg_ref[...])

        lg_t3 = _split3(lg.T[0:2 * B_HEADS, :], 0)
        nh2 = 2 * B_HEADS
        t3 = _dot(lg_t3, tri_ref[1])
        gam_t = t3[0:nh2] + t3[nh2:2 * nh2] + t3[2 * nh2:3 * nh2]
        t3 = _dot(lg_t3, tri_ref[2])
        gam_t_sw = t3[0:nh2] + t3[nh2:2 * nh2] + t3[2 * nh2:3 * nh2]
        for c in range(nc):
            tile = slice((c // 2) * LANES, (c // 2 + 1) * LANES)
            first, second = (gam_t, gam_t_sw) if c % 2 == 0 else (gam_t_sw, gam_t)
            for p in range(N_PAIRS):
                h0 = B_HEADS + 2 * p
                rows_s[b, c, p:p + 1, :] = jnp.where(low_half, first[h0:h0 + 1, tile], second[h0 + 1:h0 + 2, tile])

    ii = _iota((CHUNK, LANES), 0)
    jj = _iota((CHUNK, LANES), 1) % CHUNK
    incl = jj <= ii
    strict = jj < ii
    eye = jnp.where(jj == ii, 1.0, 0.0)
    lo_lane = _iota((CHUNK, LANES), 1) < CHUNK
    bd_mask = (_iota((LANES, LANES), 0) // CHUNK) == (_iota((LANES, LANES), 1) // CHUNK)

    def blockdiag(m):
        zero = jnp.zeros_like(m)
        return jnp.concatenate([jnp.where(lo_lane, m, zero), jnp.where(lo_lane, zero, m)], axis=0)

    groups_per_batch = nc // GROUP_CHUNKS

    def solve_group(g, carry):
        b = g // groups_per_batch
        c0 = (g % groups_per_batch) * GROUP_CHUNKS
        units = []
        for dc in range(GROUP_CHUNKS):
            c = c0 + dc
            rws = pl.ds(pl.multiple_of(c * CHUNK, CHUNK), CHUNK)
            grows = rows_s[b, c]
            for p in range(N_PAIRS):
                sl = slice(p * LANES, (p + 1) * LANES)
                q = qn_s[b, rws, sl]
                k = kn_s[b, rws, sl]
                v = v_s[b, rws, sl]
                bet = bg_s[b, rws, sl]
                gi = bg_s[b, rws, B_WIDTH + p * LANES:B_WIDTH + (p + 1) * LANES]
                dec = jnp.exp(jnp.minimum(gi - grows[p:p + 1, :], 0.0))
                eg = jnp.exp(gi)
                glast = gi[CHUNK - 1:CHUNK, :]
                k16 = k.astype(BF16)
                qk = _dot_nt(jnp.concatenate([q.astype(BF16), k16], axis=0), blockdiag(k16))
                nk = jnp.where(strict, qk[CHUNK:2 * CHUNK] * (-bet) * dec, 0.0)
                gt_s[b, c, p:p + 1, :] = jnp.exp(glast)
                units.append(dict(
                    c=c, p=p, rws=rws, sl=sl,
                    pm=jnp.where(incl, qk[0:CHUNK] * dec, 0.0).astype(BF16),
                    tm=eye + nk, nk=nk.astype(BF16),
                    rhs=jnp.concatenate([blockdiag((v * bet).astype(BF16)),
                                         blockdiag((k * (bet * eg)).astype(BF16))], axis=1),
                    qd=q * eg, kd=(k * jnp.exp(glast - gi)).astype(BF16)))
        for u in units:
            u["nk"] = _dot(u["nk"], blockdiag(u["nk"])).astype(BF16)
        for _ in range(4):
            for u in units:
                both = _dot(u["nk"], jnp.concatenate([blockdiag(u["nk"]), blockdiag(u["tm"].astype(BF16))], axis=1))
                u["nk"] = both[:, 0:LANES].astype(BF16)
                u["tm"] = u["tm"] + both[:, LANES:2 * LANES]
        for u in units:
            u["tm"] = u["tm"] + _dot(u["nk"], blockdiag(u["tm"].astype(BF16)))
        for u in units:
            u["sol"] = _dot(u["tm"].astype(BF16), u["rhs"]).astype(BF16)
        for u in units:
            ab = _dot_tn(u["kd"], u["sol"])
            bb_s[b, u["c"], u["p"]] = jnp.where(bd_mask, ab[:, 0:LANES], 0.0)
            qa_s[b, u["c"], u["p"], CHUNK:CHUNK + LANES, :] = jnp.where(bd_mask, -ab[:, LANES:2 * LANES], 0.0).astype(BF16)
        for u in units:
            sol = u["sol"]
            po = _dot(u["pm"], jnp.concatenate([blockdiag(sol[:, 0:LANES]), blockdiag(sol[:, LANES:2 * LANES])], axis=1))
            o_s[b, u["rws"], u["sl"]] = po[:, 0:LANES]
            qa_s[b, u["c"], u["p"], 0:CHUNK, :] = (u["qd"] - po[:, LANES:2 * LANES]).astype(BF16)
        return carry

    lax.fori_loop(0, nb * groups_per_batch, solve_group, 0)

    def scan_chunk(c, carry):
        rws = pl.ds(pl.multiple_of(c * CHUNK, CHUNK), CHUNK)
        chains = [(b, p) for b in range(nb) for p in range(N_PAIRS)]
        states = [s_ref[b, p] for b, p in chains]
        prods = [_dot(qa_s[b, c, p], s.astype(BF16)) for (b, p), s in zip(chains, states)]
        for (b, p), s, r in zip(chains, states, prods):
            sl = slice(p * LANES, (p + 1) * LANES)
            s_ref[b, p] = gt_s[b, c, p:p + 1, :] * s + r[CHUNK:CHUNK + LANES] + bb_s[b, c, p]
            o_s[b, rws, sl] = o_s[b, rws, sl] + r[0:CHUNK]
        return carry

    lax.fori_loop(0, nc, scan_chunk, 0)

    for b in range(nb):
        for p in range(N_PAIRS):
            sl = slice(p * LANES, (p + 1) * LANES)
            o = o_s[b, :, sl]
            ss = _group_sumsq(o, ones2)
            o_ref[b, :, sl] = (o * lax.rsqrt(ss * (1.0 / 64.0) + EPS) * ng_ref[...]
                               * _silu(z_ref[b, :, sl])).astype(o_ref.dtype)


def _chunk_triangles(tb):
    r = np.arange(tb)[:, None]
    c = np.arange(tb)[None, :]
    same = (r // CHUNK) == (c // CHUNK)
    c_sw = np.where((c % LANES) < CHUNK, c + CHUNK, c - CHUNK)
    tril = same & (c <= r)
    triu = same & (r <= c)
    triu_sw = ((r // CHUNK) == (c_sw // CHUNK)) & (r <= c_sw)
    return jnp.asarray(np.stack([tril, triu, triu_sw]).astype(np.float32), BF16)


def _gdn(h3, cw, arow, dtrow, ebg, ng, l):
    batch, seq, _ = h3.shape
    tb = min(GDN_TB, seq)
    nc = tb // CHUNK
    hspec = lambda width, col: pl.BlockSpec((batch, tb, width), lambda t: (0, t, col))
    return pl.pallas_call(
        _gdn_kernel,
        grid=(seq // tb,),
        in_specs=[
            hspec(B_WIDTH, COL_BQ), hspec(B_WIDTH, COL_BK), hspec(B_WIDTH, COL_BV), hspec(B_WIDTH, COL_BZ),
            hspec(GATE_W, COL_G),
            _layer_spec((B_CONV, 3 * B_WIDTH), l),
            _layer_spec((1, GATE_W), l),
            _layer_spec((1, GATE_W), l),
            pl.BlockSpec((2 * GATE_W, 2 * B_WIDTH), lambda t: (0, 0)),
            pl.BlockSpec((3, tb, tb), lambda t: (0, 0, 0)),
            _layer_spec((1, LANES), l),
        ],
        out_specs=pl.BlockSpec((batch, tb, B_WIDTH), lambda t: (0, t, 0)),
        out_shape=jax.ShapeDtypeStruct((batch, seq, B_WIDTH), BF16),
        scratch_shapes=[
            pltpu.VMEM((batch, tb + TAIL, 3 * B_WIDTH), F32),
            pltpu.VMEM((batch, tb, B_WIDTH), F32),
            pltpu.VMEM((batch, tb, B_WIDTH), F32),
            pltpu.VMEM((batch, tb, B_WIDTH), F32),
            pltpu.VMEM((batch, tb, 2 * B_WIDTH), F32),
            pltpu.VMEM((batch, nc, 8, LANES), F32),
            pltpu.VMEM((batch, nc, N_PAIRS, CHUNK + LANES, LANES), BF16),
            pltpu.VMEM((batch, nc, N_PAIRS, LANES, LANES), F32),
            pltpu.VMEM((batch, nc, 8, LANES), F32),
            pltpu.VMEM((batch, tb, B_WIDTH), F32),
            pltpu.VMEM((batch, N_PAIRS, LANES, LANES), F32),
        ],
        compiler_params=pltpu.CompilerParams(dimension_semantics=("arbitrary",), vmem_limit_bytes=VMEM_LIMIT),
        name="gdn",
    )(h3, h3, h3, h3, h3, cw, arow, dtrow, ebg, _chunk_triangles(tb), ng)


OUT_TM = 512


def _out_ln_kernel(x_ref, a_ref, b_ref, c_ref, w_ref, g_ref, bb_ref, o_ref):
    mix = _dot(a_ref[...], w_ref[0:A_WIDTH, :])
    mix = mix + _dot(b_ref[...], w_ref[A_WIDTH:A_WIDTH + B_WIDTH, :])
    mix = mix + _dot(c_ref[...], w_ref[A_WIDTH + B_WIDTH:, :])
    o_ref[...] = _layer_norm(ALPHA * x_ref[...] + mix, g_ref[...], bb_ref[...])


def _out_ln(x, a, b, c, w, g, bb, l):
    n = x.shape[0]
    tm = min(OUT_TM, n)
    return pl.pallas_call(
        _out_ln_kernel,
        grid=(n // tm,),
        in_specs=[
            pl.BlockSpec((tm, D_MODEL), lambda i: (i, 0)),
            pl.BlockSpec((tm, A_WIDTH), lambda i: (i, 0)),
            pl.BlockSpec((tm, B_WIDTH), lambda i: (i, 0)),
            pl.BlockSpec((tm, C_WIDTH), lambda i: (i, 0)),
            _layer_spec((D_MODEL, D_MODEL), l),
            _layer_spec((1, D_MODEL), l),
            _layer_spec((1, D_MODEL), l),
        ],
        out_specs=pl.BlockSpec((tm, D_MODEL), lambda i: (i, 0)),
        out_shape=jax.ShapeDtypeStruct((n, D_MODEL), F32),
        compiler_params=pltpu.CompilerParams(dimension_semantics=("arbitrary",), vmem_limit_bytes=VMEM_LIMIT),
        name="out_ln",
    )(x, a, b, c, w, g, bb)


def _pad_w_in(w_in):
    o = np.cumsum([0, A_WIDTH, A_WIDTH, 3 * B_WIDTH, B_WIDTH, B_HEADS, B_HEADS, C_KEY, C_KEY, C_WIDTH, C_WIDTH, C_RANK])
    a_u, a_v, b_qkv, b_z, b_beta, b_dec, c_q, c_k, c_v, c_r, c_g = (w_in[..., o[i]:o[i + 1]] for i in range(11))
    pad = jnp.zeros(w_in.shape[:-1] + (GATE_W - 2 * B_HEADS - C_RANK,), w_in.dtype)
    return jnp.concatenate([a_u, a_v, b_qkv, b_z, c_v, c_r, c_q, c_k, b_beta, b_dec, c_g, pad], axis=-1).astype(BF16)


def _bcast_selector():
    e = np.zeros((2 * GATE_W, 2 * B_WIDTH), np.float32)
    for j in range(B_WIDTH):
        hd = j // 64
        for base in (0, GATE_W):
            e[base + hd, j] = 1.0
            e[base + B_HEADS + hd, B_WIDTH + j] = 1.0
    return jnp.asarray(e, BF16)


def _gate_row(vec, offset):
    l, n = vec.shape
    return jnp.zeros((l, 1, GATE_W), F32).at[:, 0, offset:offset + n].set(vec)


def kernel(x, ffn1_w1, ffn1_w3, ffn1_w2, ln1_g, ln1_b, w_in, sgu_ln_g, sgu_ln_b, sgu_w, sgu_b, gdn_conv_w, gdn_a_log, gdn_dt_bias, gdn_norm_g, gla_gate_up, gla_gate_b, gla_norm_g, w_out, ln2_g, ln2_b, ffn2_w1, ffn2_w3, ffn2_w2, ln3_g, ln3_b):
    batch, seq, d = x.shape
    depth = w_in.shape[0]
    row = lambda p: p[:, None, :]
    f1 = [w.astype(BF16) for w in (ffn1_w1, ffn1_w3, ffn1_w2)]
    f2 = [w.astype(BF16) for w in (ffn2_w1, ffn2_w3, ffn2_w2)]
    w_in_p = _pad_w_in(w_in)
    w_out_b = w_out.astype(BF16)
    sgu_bias = jnp.repeat(jnp.swapaxes(sgu_b, 1, 2), HEAD_DIM, axis=2)
    arow = _gate_row(gdn_a_log, B_HEADS)
    dtrow = _gate_row(gdn_dt_bias, B_HEADS)
    ebg = _bcast_selector()
    gdn_ng = jnp.tile(gdn_norm_g, (1, 2))[:, None, :]
    gla_ng = jnp.tile(gla_norm_g, (1, C_HEADS))[:, None, :]
    gup = jnp.zeros((depth, GATE_W, C_KEY), F32).at[:, 2 * B_HEADS:2 * B_HEADS + C_RANK, :].set(gla_gate_up)

    xf = x.reshape(batch * seq, d)
    for l in range(depth):
        xf = _ffn_ln(xf, *f1, row(ln1_g), row(ln1_b), l)
        h = _proj(xf, w_in_p, l)
        out_a = _sgu(h, row(sgu_ln_g), row(sgu_ln_b), sgu_w, sgu_bias, l)
        out_b = _gdn(h.reshape(batch, seq, D_IN_PAD), gdn_conv_w, arow, dtrow, ebg, gdn_ng, l)
        out_c = _gla(h, batch, seq, gup, row(gla_gate_b), gla_ng, l)
        xf = _out_ln(xf, out_a, out_b.reshape(batch * seq, B_WIDTH), out_c, w_out_b, row(ln2_g), row(ln2_b), l)
        xf = _ffn_ln(xf, *f2, row(ln3_g), row(ln3_b), l)
    return xf.reshape(batch, seq, d)
```

```python
import numpy as np
import jax
import jax.numpy as jnp
from jax import lax
from jax.experimental import pallas as pl
from jax.experimental.pallas import tpu as pltpu

F32 = jnp.float32
BF16 = jnp.bfloat16
HIGHEST = lax.Precision.HIGHEST

D_MODEL = 1024
DEPTH = 4
HEAD_DIM = 64
A_HEADS = 4
A_WIDTH = A_HEADS * HEAD_DIM
A_CHUNK = 128
B_HEADS = 8
B_DK = 64
B_WIDTH = B_HEADS * 64
B_CONV = 4
C_HEADS = 4
C_DK = 32
C_DV = 64
C_KEY = C_HEADS * C_DK
C_WIDTH = C_HEADS * C_DV
C_RANK = 16
C_TAU = 16.0
D_FF = 2816
ALPHA = (2.0 * DEPTH) ** 0.25
EPS = 1e-5

LANES = 128
CHUNK = 64
GATE_W = LANES
D_IN_PAD = 2 * A_WIDTH + 4 * B_WIDTH + 2 * C_WIDTH + 2 * C_KEY + GATE_W
VMEM_LIMIT = 56 * 1024 * 1024

D_H = D_IN_PAD - 2 * A_WIDTH

COL_BQ, COL_BK, COL_BV, COL_BZ = 0, 1, 2, 3
COL_CV, COL_CR = 8, 9
COL_CQ, COL_CK, COL_G = 20, 21, 22


def _dot(a, b, **kw):
    return jnp.dot(a, b, preferred_element_type=F32, **kw)


def _dot_nt(a, b, **kw):
    return lax.dot_general(a, b, (((1,), (1,)), ((), ())), preferred_element_type=F32, **kw)


def _dot_tn(a, b, **kw):
    return lax.dot_general(a, b, (((0,), (0,)), ((), ())), preferred_element_type=F32, **kw)


def _split2(x):
    hi = x.astype(BF16)
    lo = (x - hi.astype(F32)).astype(BF16)
    return jnp.concatenate([hi, lo], axis=-1)


def _split3(x, axis):
    hi = x.astype(BF16)
    r1 = x - hi.astype(F32)
    mid = r1.astype(BF16)
    lo = (r1 - mid.astype(F32)).astype(BF16)
    return jnp.concatenate([hi, mid, lo], axis=axis)


def _iota(shape, dim):
    return lax.broadcasted_iota(jnp.int32, shape, dim)


def _group_ones(width, group):
    r = _iota((width, width), 0)
    c = _iota((width, width), 1)
    return jnp.where((r // group) == (c // group), 1.0, 0.0).astype(BF16)


def _group_sumsq(x, ones):
    return _dot((x * x).astype(BF16), ones)


def _layer_norm(y, g, b):
    mu = jnp.mean(y, axis=-1, keepdims=True)
    yc = y - mu
    var = jnp.mean(yc * yc, axis=-1, keepdims=True)
    return yc * lax.rsqrt(var + EPS) * g + b


def _silu(x):
    return x * jax.nn.sigmoid(x)


def _gelu_tanh(x):
    return 0.5 * x * (1.0 + jnp.tanh(0.7978845608028654 * (x + 0.044715 * (x * x * x))))


def _softplus(x):
    return jnp.maximum(x, 0.0) + jnp.log(1.0 + jnp.exp(-jnp.abs(x)))


FFN_TM = 512
FFN_TF = 256


def _ffn_ln_body(x, w1_ref, w3_ref, w2_ref, g_ref, b_ref, a_ref):
    xb = x.astype(BF16)
    for j in range(D_FF // FFN_TF):
        sl = slice(j * FFN_TF, (j + 1) * FFN_TF)
        h1 = _dot(xb, w1_ref[:, sl])
        h3 = _dot(xb, w3_ref[:, sl])
        a_ref[:, sl] = (_silu(h1) * h3).astype(BF16)
    y = ALPHA * x + 0.5 * _dot(a_ref[...], w2_ref[...])
    return _layer_norm(y, g_ref[...], b_ref[...])


def _ffn_ln_kernel(x_ref, w1_ref, w3_ref, w2_ref, g_ref, b_ref, o_ref, a_ref):
    o_ref[...] = _ffn_ln_body(x_ref[...], w1_ref, w3_ref, w2_ref, g_ref, b_ref, a_ref)


def _layer_spec(shape, l, **kw):
    zeros = (0,) * len(shape)
    return pl.BlockSpec((None,) + tuple(shape), lambda *_: (l,) + zeros, **kw)


def _ffn_ln(x, w1, w3, w2, g, b, l):
    n = x.shape[0]
    tm = min(FFN_TM, n)
    return pl.pallas_call(
        _ffn_ln_kernel,
        grid=(n // tm,),
        in_specs=[
            pl.BlockSpec((tm, D_MODEL), lambda i: (i, 0)),
            _layer_spec((D_MODEL, D_FF), l, pipeline_mode=pl.Buffered(1)),
            _layer_spec((D_MODEL, D_FF), l, pipeline_mode=pl.Buffered(1)),
            _layer_spec((D_FF, D_MODEL), l, pipeline_mode=pl.Buffered(1)),
            _layer_spec((1, D_MODEL), l),
            _layer_spec((1, D_MODEL), l),
        ],
        out_specs=pl.BlockSpec((tm, D_MODEL), lambda i: (i, 0)),
        out_shape=jax.ShapeDtypeStruct((n, D_MODEL), F32),
        scratch_shapes=[pltpu.VMEM((tm, D_FF), BF16)],
        compiler_params=pltpu.CompilerParams(dimension_semantics=("arbitrary",), vmem_limit_bytes=VMEM_LIMIT),
        name="ffn_ln",
    )(x, w1, w3, w2, g, b)


PROJ_TM = 512


def _proj_sgu_kernel(x_ref, w_ref, lng_ref, lnb_ref, sw_ref, bias_ref, h_ref, a_ref):
    tm = x_ref.shape[0]
    xb = x_ref[...].astype(BF16)
    h_ref[...] = _dot(xb, w_ref[:, 2 * A_WIDTH:])
    ha = _dot(xb, w_ref[:, 0:2 * A_WIDTH])
    u = _gelu_tanh(ha[:, 0:A_WIDTH])
    v = _layer_norm(_gelu_tanh(ha[:, A_WIDTH:2 * A_WIDTH]), lng_ref[...], lnb_ref[...])
    tril = _iota((A_CHUNK, A_CHUNK), 1) <= _iota((A_CHUNK, A_CHUNK), 0)
    head_of_lane = _iota((A_CHUNK, A_WIDTH), 1) // HEAD_DIM
    wts = [jnp.where(tril, sw_ref[h], 0.0).astype(BF16) for h in range(A_HEADS)]
    bias = bias_ref[...]
    for c in range(tm // A_CHUNK):
        rows = slice(c * A_CHUNK, (c + 1) * A_CHUNK)
        vc = v[rows]
        z = bias
        for h in range(A_HEADS):
            z = z + _dot(wts[h], jnp.where(head_of_lane == h, vc, 0.0).astype(BF16))
        a_ref[rows, :] = (u[rows] * z).astype(a_ref.dtype)


def _proj_sgu(x, w, lng, lnb, sw, bias, l):
    n = x.shape[0]
    tm = min(PROJ_TM, n)
    return pl.pallas_call(
        _proj_sgu_kernel,
        grid=(n // tm,),
        in_specs=[
            pl.BlockSpec((tm, D_MODEL), lambda i: (i, 0)),
            _layer_spec((D_MODEL, D_IN_PAD), l, pipeline_mode=pl.Buffered(1)),
            _layer_spec((1, A_WIDTH), l),
            _layer_spec((1, A_WIDTH), l),
            _layer_spec((A_HEADS, A_CHUNK, A_CHUNK), l),
            _layer_spec((A_CHUNK, A_WIDTH), l),
        ],
        out_specs=[pl.BlockSpec((tm, D_H), lambda i: (i, 0)), pl.BlockSpec((tm, A_WIDTH), lambda i: (i, 0))],
        out_shape=[jax.ShapeDtypeStruct((n, D_H), F32), jax.ShapeDtypeStruct((n, A_WIDTH), BF16)],
        compiler_params=pltpu.CompilerParams(dimension_semantics=("arbitrary",), vmem_limit_bytes=VMEM_LIMIT),
        name="proj_sgu",
    )(x, w, lng, lnb, sw, bias)


GLA_TB = 256


def _gla_kernel(q_ref, k_ref, v_ref, r_ref, g_ref, gup_ref, gb_ref, tri_ref, ng_ref, o_ref, st_ref, ob_ref):
    nb, tb = q_ref.shape[0], q_ref.shape[1]
    nc = tb // CHUNK

    @pl.when(pl.program_id(0) == 0)
    def _():
        st_ref[...] = jnp.zeros_like(st_ref)

    khead = _iota((CHUNK, C_KEY), 1) // C_DK
    vhead = _iota((CHUNK, C_WIDTH), 1) // C_DV
    causal = (_iota((CHUNK, C_WIDTH), 1) % CHUNK) <= _iota((CHUNK, C_WIDTH), 0)
    st_mask = (_iota((C_WIDTH, C_KEY), 0) // C_DV) == (_iota((C_WIDTH, C_KEY), 1) // C_DK)

    units = []
    for b in range(nb):
        logit = _dot(g_ref[b], gup_ref[...], precision=HIGHEST) + gb_ref[...]
        log_a = -_softplus(-logit) * (1.0 / C_TAU)
        g3 = _dot(tri_ref[...], _split3(log_a, 1))
        bc_all = g3[:, 0:C_KEY] + g3[:, C_KEY:2 * C_KEY] + g3[:, 2 * C_KEY:3 * C_KEY]
        for c in range(nc):
            rows = slice(c * CHUNK, (c + 1) * CHUNK)
            bc = bc_all[rows]
            btot = bc[CHUNK - 1:CHUNK, :]
            k = k_ref[b, rows, :]
            kn = k * jnp.exp(-bc)
            vb = v_ref[b, rows, :].astype(BF16)
            units.append(dict(
                b=b, rows=rows, vb=vb, gt=jnp.exp(btot),
                qd=(q_ref[b, rows, :] * ((C_DK ** -0.5) * jnp.exp(bc))).astype(BF16),
                kd=(k * jnp.exp(btot - bc)).astype(BF16),
                kstack=jnp.concatenate([jnp.where(khead == h, kn, 0.0) for h in range(C_HEADS)], axis=0).astype(BF16),
                vbd=jnp.concatenate([jnp.where(vhead == h, vb, jnp.zeros_like(vb)) for h in range(C_HEADS)], axis=0)))
    for u in units:
        u["p"] = jnp.where(causal, _dot_nt(u["qd"], u["kstack"]), 0.0).astype(BF16)
    for u in units:
        u["upd"] = jnp.where(st_mask, _dot_tn(u["vb"], u["kd"]), 0.0)
    for u in units:
        u["oi"] = _dot(u["p"], u["vbd"])

    states = [st_ref[b] for b in range(nb)]
    for u in units:
        b = u["b"]
        ob_ref[b, u["rows"], :] = u["oi"] + _dot_nt(u["qd"], states[b].astype(BF16))
        states[b] = u["gt"] * states[b] + u["upd"]
    for b in range(nb):
        st_ref[b] = states[b]

    ones = _group_ones(C_WIDTH, C_DV)
    for b in range(nb):
        o = ob_ref[b]
        ss = _group_sumsq(o, ones)
        o_ref[b] = (o * lax.rsqrt(ss * (1.0 / C_DV) + EPS) * ng_ref[...] * _silu(r_ref[b])).astype(o_ref.dtype)


def _gla(h3, gup, gb, ng, l):
    batch, seq, _ = h3.shape
    tb = min(GLA_TB, seq)
    hspec = lambda width, col: pl.BlockSpec((batch, tb, width), lambda t: (0, t, col))
    return pl.pallas_call(
        _gla_kernel,
        grid=(seq // tb,),
        in_specs=[
            hspec(C_KEY, COL_CQ), hspec(C_KEY, COL_CK), hspec(C_WIDTH, COL_CV), hspec(C_WIDTH, COL_CR),
            hspec(GATE_W, COL_G),
            _layer_spec((GATE_W, C_KEY), l),
            _layer_spec((1, C_KEY), l),
            pl.BlockSpec((tb, tb), lambda t: (0, 0)),
            _layer_spec((1, C_WIDTH), l),
        ],
        out_specs=pl.BlockSpec((batch, tb, C_WIDTH), lambda t: (0, t, 0)),
        out_shape=jax.ShapeDtypeStruct((batch, seq, C_WIDTH), BF16),
        scratch_shapes=[pltpu.VMEM((batch, C_WIDTH, C_KEY), F32), pltpu.VMEM((batch, tb, C_WIDTH), F32)],
        compiler_params=pltpu.CompilerParams(dimension_semantics=("arbitrary",), vmem_limit_bytes=VMEM_LIMIT),
        name="gla",
    )(h3, h3, h3, h3, h3, gup, gb, _chunk_triangles(tb)[0], ng)


GDN_TB = 256
N_PAIRS = B_HEADS // 2
TAIL = 8
GROUP_CHUNKS = 4


def _gdn_kernel(q_ref, k_ref, v_ref, z_ref, g_ref, cw_ref, arow_ref, dtrow_ref, ebg_ref, tri_ref, ng_ref, o_ref,
                xbuf, qn_s, kn_s, v_s, bg_s, rows_s, qa_s, bb_s, gt_s, o_s, s_ref):
    nb, tb = q_ref.shape[0], q_ref.shape[1]
    nc = tb // CHUNK
    w3 = 3 * B_WIDTH

    @pl.when(pl.program_id(0) == 0)
    def _():
        s_ref[...] = jnp.zeros_like(s_ref)
        xbuf[:, 0:TAIL, :] = jnp.zeros((nb, TAIL, w3), F32)

    ones2 = _group_ones(LANES, B_DK)
    lane = _iota((tb, GATE_W), 1)
    low_half = _iota((1, LANES), 1) < CHUNK
    for b in range(nb):
        xbuf[b, TAIL:TAIL + tb, 0:B_WIDTH] = q_ref[b]
        xbuf[b, TAIL:TAIL + tb, B_WIDTH:2 * B_WIDTH] = k_ref[b]
        xbuf[b, TAIL:TAIL + tb, 2 * B_WIDTH:w3] = v_ref[b]
        y = cw_ref[B_CONV - 1:B_CONV, :] * xbuf[b, TAIL:TAIL + tb, :]
        for i in range(B_CONV - 1):
            back = B_CONV - 1 - i
            y = y + cw_ref[i:i + 1, :] * xbuf[b, TAIL - back:TAIL - back + tb, :]
        xbuf[b, 0:TAIL, :] = xbuf[b, tb:tb + TAIL, :]
        y = _silu(y)
        for p in range(N_PAIRS):
            sl = slice(p * LANES, (p + 1) * LANES)
            yq = y[:, sl]
            qn_s[b, :, sl] = yq * (lax.rsqrt(_group_sumsq(yq, ones2) + 1e-6) * (B_DK ** -0.5))
            yk = y[:, B_WIDTH + p * LANES:B_WIDTH + (p + 1) * LANES]
            kn_s[b, :, sl] = yk * lax.rsqrt(_group_sumsq(yk, ones2) + 1e-6)
        v_s[b] = y[:, 2 * B_WIDTH:w3]

        gates = g_ref[b]
        beta = jax.nn.sigmoid(gates)
        lg = jnp.where((lane >= B_HEADS) & (lane < 2 * B_HEADS),
                       -jnp.exp(arow_ref[...]) * _softplus(gates + dtrow_ref[...]), 0.0)
        g3 = _dot(tri_ref[0], _split3(lg, 1))
        gam = g3[:, 0:GATE_W] + g3[:, GATE_W:2 * GATE_W] + g3[:, 2 * GATE_W:3 * GATE_W]
        feat = jnp.where(lane < B_HEADS, beta, gam)
        bg_s[b] = _dot(_split2(feat), ebg_ref[...])

        lg_t3 = _split3(lg.T[0:2 * B_HEADS, :], 0)
        nh2 = 2 * B_HEADS
        t3 = _dot(lg_t3, tri_ref[1])
        gam_t = t3[0:nh2] + t3[nh2:2 * nh2] + t3[2 * nh2:3 * nh2]
        t3 = _dot(lg_t3, tri_ref[2])
        gam_t_sw = t3[0:nh2] + t3[nh2:2 * nh2] + t3[2 * nh2:3 * nh2]
        for c in range(nc):
            tile = slice((c // 2) * LANES, (c // 2 + 1) * LANES)
            first, second = (gam_t, gam_t_sw) if c % 2 == 0 else (gam_t_sw, gam_t)
            for p in range(N_PAIRS):
                h0 = B_HEADS + 2 * p
                rows_s[b, c, p:p + 1, :] = jnp.where(low_half, first[h0:h0 + 1, tile], second[h0 + 1:h0 + 2, tile])

    ii = _iota((CHUNK, LANES), 0)
    jj = _iota((CHUNK, LANES), 1) % CHUNK
    incl = jj <= ii
    strict = jj < ii
    eye = jnp.where(jj == ii, 1.0, 0.0)
    lo_lane = _iota((CHUNK, LANES), 1) < CHUNK
    bd_mask = (_iota((LANES, LANES), 0) // CHUNK) == (_iota((LANES, LANES), 1) // CHUNK)

    def blockdiag(m):
        zero = jnp.zeros_like(m)
        return jnp.concatenate([jnp.where(lo_lane, m, zero), jnp.where(lo_lane, zero, m)], axis=0)

    groups_per_batch = nc // GROUP_CHUNKS

    def solve_group(g, carry):
        b = g // groups_per_batch
        c0 = (g % groups_per_batch) * GROUP_CHUNKS
        units = []
        for dc in range(GROUP_CHUNKS):
            c = c0 + dc
            rws = pl.ds(pl.multiple_of(c * CHUNK, CHUNK), CHUNK)
            grows = rows_s[b, c]
            for p in range(N_PAIRS):
                sl = slice(p * LANES, (p + 1) * LANES)
                q = qn_s[b, rws, sl]
                k = kn_s[b, rws, sl]
                v = v_s[b, rws, sl]
                bet = bg_s[b, rws, sl]
                gi = bg_s[b, rws, B_WIDTH + p * LANES:B_WIDTH + (p + 1) * LANES]
                dec = jnp.exp(jnp.minimum(gi - grows[p:p + 1, :], 0.0))
                eg = jnp.exp(gi)
                glast = gi[CHUNK - 1:CHUNK, :]
                k16 = k.astype(BF16)
                qk = _dot_nt(jnp.concatenate([q.astype(BF16), k16], axis=0), blockdiag(k16))
                nk = jnp.where(strict, qk[CHUNK:2 * CHUNK] * (-bet) * dec, 0.0)
                gt_s[b, c, p:p + 1, :] = jnp.exp(glast)
                units.append(dict(
                    c=c, p=p, rws=rws, sl=sl,
                    pm=jnp.where(incl, qk[0:CHUNK] * dec, 0.0).astype(BF16),
                    tm=eye + nk, nk=nk.astype(BF16),
                    rhs=jnp.concatenate([blockdiag((v * bet).astype(BF16)),
                                         blockdiag((k * (bet * eg)).astype(BF16))], axis=1),
                    qd=q * eg, kd=(k * jnp.exp(glast - gi)).astype(BF16)))
        for u in units:
            u["nk"] = _dot(u["nk"], blockdiag(u["nk"])).astype(BF16)
        for _ in range(4):
            for u in units:
                both = _dot(u["nk"], jnp.concatenate([blockdiag(u["nk"]), blockdiag(u["tm"].astype(BF16))], axis=1))
                u["nk"] = both[:, 0:LANES].astype(BF16)
                u["tm"] = u["tm"] + both[:, LANES:2 * LANES]
        for u in units:
            u["tm"] = u["tm"] + _dot(u["nk"], blockdiag(u["tm"].astype(BF16)))
        for u in units:
            u["sol"] = _dot(u["tm"].astype(BF16), u["rhs"]).astype(BF16)
        for u in units:
            ab = _dot_tn(u["kd"], u["sol"])
            bb_s[b, u["c"], u["p"]] = jnp.where(bd_mask, ab[:, 0:LANES], 0.0)
            qa_s[b, u["c"], u["p"], CHUNK:CHUNK + LANES, :] = jnp.where(bd_mask, -ab[:, LANES:2 * LANES], 0.0).astype(BF16)
        for u in units:
            sol = u["sol"]
            po = _dot(u["pm"], jnp.concatenate([blockdiag(sol[:, 0:LANES]), blockdiag(sol[:, LANES:2 * LANES])], axis=1))
            o_s[b, u["rws"], u["sl"]] = po[:, 0:LANES]
            qa_s[b, u["c"], u["p"], 0:CHUNK, :] = (u["qd"] - po[:, LANES:2 * LANES]).astype(BF16)
        return carry

    lax.fori_loop(0, nb * groups_per_batch, solve_group, 0)

    def scan_chunk(c, carry):
        rws = pl.ds(pl.multiple_of(c * CHUNK, CHUNK), CHUNK)
        chains = [(b, p) for b in range(nb) for p in range(N_PAIRS)]
        states = [s_ref[b, p] for b, p in chains]
        prods = [_dot(qa_s[b, c, p], s.astype(BF16)) for (b, p), s in zip(chains, states)]
        for (b, p), s, r in zip(chains, states, prods):
            sl = slice(p * LANES, (p + 1) * LANES)
            s_ref[b, p] = gt_s[b, c, p:p + 1, :] * s + r[CHUNK:CHUNK + LANES] + bb_s[b, c, p]
            o_s[b, rws, sl] = o_s[b, rws, sl] + r[0:CHUNK]
        return carry

    lax.fori_loop(0, nc, scan_chunk, 0)

    for b in range(nb):
        for p in range(N_PAIRS):
            sl = slice(p * LANES, (p + 1) * LANES)
            o = o_s[b, :, sl]
            ss = _group_sumsq(o, ones2)
            o_ref[b, :, sl] = (o * lax.rsqrt(ss * (1.0 / 64.0) + EPS) * ng_ref[...]
                               * _silu(z_ref[b, :, sl])).astype(o_ref.dtype)


def _chunk_triangles(tb):
    r = np.arange(tb)[:, None]
    c = np.arange(tb)[None, :]
    same = (r // CHUNK) == (c // CHUNK)
    c_sw = np.where((c % LANES) < CHUNK, c + CHUNK, c - CHUNK)
    tril = same & (c <= r)
    triu = same & (r <= c)
    triu_sw = ((r // CHUNK) == (c_sw // CHUNK)) & (r <= c_sw)
    return jnp.asarray(np.stack([tril, triu, triu_sw]).astype(np.float32), BF16)


def _gdn(h3, cw, arow, dtrow, ebg, ng, l):
    batch, seq, _ = h3.shape
    tb = min(GDN_TB, seq)
    nc = tb // CHUNK
    hspec = lambda width, col: pl.BlockSpec((batch, tb, width), lambda t: (0, t, col))
    return pl.pallas_call(
        _gdn_kernel,
        grid=(seq // tb,),
        in_specs=[
            hspec(B_WIDTH, COL_BQ), hspec(B_WIDTH, COL_BK), hspec(B_WIDTH, COL_BV), hspec(B_WIDTH, COL_BZ),
            hspec(GATE_W, COL_G),
            _layer_spec((B_CONV, 3 * B_WIDTH), l),
            _layer_spec((1, GATE_W), l),
            _layer_spec((1, GATE_W), l),
            pl.BlockSpec((2 * GATE_W, 2 * B_WIDTH), lambda t: (0, 0)),
            pl.BlockSpec((3, tb, tb), lambda t: (0, 0, 0)),
            _layer_spec((1, LANES), l),
        ],
        out_specs=pl.BlockSpec((batch, tb, B_WIDTH), lambda t: (0, t, 0)),
        out_shape=jax.ShapeDtypeStruct((batch, seq, B_WIDTH), BF16),
        scratch_shapes=[
            pltpu.VMEM((batch, tb + TAIL, 3 * B_WIDTH), F32),
            pltpu.VMEM((batch, tb, B_WIDTH), F32),
            pltpu.VMEM((batch, tb, B_WIDTH), F32),
            pltpu.VMEM((batch, tb, B_WIDTH), F32),
            pltpu.VMEM((batch, tb, 2 * B_WIDTH), F32),
            pltpu.VMEM((batch, nc, 8, LANES), F32),
            pltpu.VMEM((batch, nc, N_PAIRS, CHUNK + LANES, LANES), BF16),
            pltpu.VMEM((batch, nc, N_PAIRS, LANES, LANES), F32),
            pltpu.VMEM((batch, nc, 8, LANES), F32),
            pltpu.VMEM((batch, tb, B_WIDTH), F32),
            pltpu.VMEM((batch, N_PAIRS, LANES, LANES), F32),
        ],
        compiler_params=pltpu.CompilerParams(dimension_semantics=("arbitrary",), vmem_limit_bytes=VMEM_LIMIT),
        name="gdn",
    )(h3, h3, h3, h3, h3, cw, arow, dtrow, ebg, _chunk_triangles(tb), ng)


def _out_ffn_kernel(x_ref, a_ref, b_ref, c_ref, wo_ref, g2_ref, b2_ref, w1_ref, w3_ref, w2_ref, g3_ref, b3_ref,
                    o_ref, act_ref):
    mix = _dot(a_ref[...], wo_ref[0:A_WIDTH, :])
    mix = mix + _dot(b_ref[...], wo_ref[A_WIDTH:A_WIDTH + B_WIDTH, :])
    mix = mix + _dot(c_ref[...], wo_ref[A_WIDTH + B_WIDTH:, :])
    x2 = _layer_norm(ALPHA * x_ref[...] + mix, g2_ref[...], b2_ref[...])
    o_ref[...] = _ffn_ln_body(x2, w1_ref, w3_ref, w2_ref, g3_ref, b3_ref, act_ref)


def _out_ffn(x, a, b, c, wo, g2, b2, w1, w3, w2, g3, b3, l):
    n = x.shape[0]
    tm = min(FFN_TM, n)
    rows = lambda width: pl.BlockSpec((tm, width), lambda i: (i, 0))
    return pl.pallas_call(
        _out_ffn_kernel,
        grid=(n // tm,),
        in_specs=[
            rows(D_MODEL), rows(A_WIDTH), rows(B_WIDTH), rows(C_WIDTH),
            _layer_spec((D_MODEL, D_MODEL), l, pipeline_mode=pl.Buffered(1)),
            _layer_spec((1, D_MODEL), l),
            _layer_spec((1, D_MODEL), l),
            _layer_spec((D_MODEL, D_FF), l, pipeline_mode=pl.Buffered(1)),
            _layer_spec((D_MODEL, D_FF), l, pipeline_mode=pl.Buffered(1)),
            _layer_spec((D_FF, D_MODEL), l, pipeline_mode=pl.Buffered(1)),
            _layer_spec((1, D_MODEL), l),
            _layer_spec((1, D_MODEL), l),
        ],
        out_specs=rows(D_MODEL),
        out_shape=jax.ShapeDtypeStruct((n, D_MODEL), F32),
        scratch_shapes=[pltpu.VMEM((tm, D_FF), BF16)],
        compiler_params=pltpu.CompilerParams(dimension_semantics=("arbitrary",), vmem_limit_bytes=VMEM_LIMIT),
        name="out_ffn",
    )(x, a, b, c, wo, g2, b2, w1, w3, w2, g3, b3)


def _pad_w_in(w_in):
    o = np.cumsum([0, A_WIDTH, A_WIDTH, 3 * B_WIDTH, B_WIDTH, B_HEADS, B_HEADS, C_KEY, C_KEY, C_WIDTH, C_WIDTH, C_RANK])
    a_u, a_v, b_qkv, b_z, b_beta, b_dec, c_q, c_k, c_v, c_r, c_g = (w_in[..., o[i]:o[i + 1]] for i in range(11))
    pad = jnp.zeros(w_in.shape[:-1] + (GATE_W - 2 * B_HEADS - C_RANK,), w_in.dtype)
    return jnp.concatenate([a_u, a_v, b_qkv, b_z, c_v, c_r, c_q, c_k, b_beta, b_dec, c_g, pad], axis=-1).astype(BF16)


def _bcast_selector():
    e = np.zeros((2 * GATE_W, 2 * B_WIDTH), np.float32)
    for j in range(B_WIDTH):
        hd = j // 64
        for base in (0, GATE_W):
            e[base + hd, j] = 1.0
            e[base + B_HEADS + hd, B_WIDTH + j] = 1.0
    return jnp.asarray(e, BF16)


def _gate_row(vec, offset):
    l, n = vec.shape
    return jnp.zeros((l, 1, GATE_W), F32).at[:, 0, offset:offset + n].set(vec)


def kernel(x, ffn1_w1, ffn1_w3, ffn1_w2, ln1_g, ln1_b, w_in, sgu_ln_g, sgu_ln_b, sgu_w, sgu_b, gdn_conv_w, gdn_a_log, gdn_dt_bias, gdn_norm_g, gla_gate_up, gla_gate_b, gla_norm_g, w_out, ln2_g, ln2_b, ffn2_w1, ffn2_w3, ffn2_w2, ln3_g, ln3_b):
    batch, seq, d = x.shape
    depth = w_in.shape[0]
    row = lambda p: p[:, None, :]
    f1 = [w.astype(BF16) for w in (ffn1_w1, ffn1_w3, ffn1_w2)]
    f2 = [w.astype(BF16) for w in (ffn2_w1, ffn2_w3, ffn2_w2)]
    w_in_p = _pad_w_in(w_in)
    w_out_b = w_out.astype(BF16)
    sgu_bias = jnp.repeat(jnp.swapaxes(sgu_b, 1, 2), HEAD_DIM, axis=2)
    arow = _gate_row(gdn_a_log, B_HEADS)
    dtrow = _gate_row(gdn_dt_bias, B_HEADS)
    ebg = _bcast_selector()
    gdn_ng = jnp.tile(gdn_norm_g, (1, 2))[:, None, :]
    gla_ng = jnp.tile(gla_norm_g, (1, C_HEADS))[:, None, :]
    gup = jnp.zeros((depth, GATE_W, C_KEY), F32).at[:, 2 * B_HEADS:2 * B_HEADS + C_RANK, :].set(gla_gate_up)

    xf = x.reshape(batch * seq, d)
    for l in range(depth):
        xf = _ffn_ln(xf, *f1, row(ln1_g), row(ln1_b), l)
        h, out_a = _proj_sgu(xf, w_in_p, row(sgu_ln_g), row(sgu_ln_b), sgu_w, sgu_bias, l)
        h3 = h.reshape(batch, seq, D_H)
        out_b = _gdn(h3, gdn_conv_w, arow, dtrow, ebg, gdn_ng, l).reshape(batch * seq, B_WIDTH)
        out_c = _gla(h3, gup, row(gla_gate_b), gla_ng, l).reshape(batch * seq, C_WIDTH)
        xf = _out_ffn(xf, out_a, out_b, out_c, w_out_b, row(ln2_g), row(ln2_b), *f2, row(ln3_g), row(ln3_b), l)
    return xf.reshape(batch, seq, d)
```

```python
import functools

import numpy as np
import jax
import jax.numpy as jnp
from jax import lax
from jax.experimental import pallas as pl
from jax.experimental.pallas import tpu as pltpu

F32 = jnp.float32
BF16 = jnp.bfloat16
HIGHEST = lax.Precision.HIGHEST

D_MODEL = 1024
DEPTH = 4
HEAD_DIM = 64
A_HEADS = 4
A_WIDTH = A_HEADS * HEAD_DIM
A_CHUNK = 128
B_HEADS = 8
B_DK = 64
B_WIDTH = B_HEADS * 64
B_CONV = 4
C_HEADS = 4
C_DK = 32
C_DV = 64
C_KEY = C_HEADS * C_DK
C_WIDTH = C_HEADS * C_DV
C_RANK = 16
C_TAU = 16.0
D_FF = 2816
ALPHA = (2.0 * DEPTH) ** 0.25
EPS = 1e-5

LANES = 128
CHUNK = 64
GATE_W = LANES
N_PAIRS = B_HEADS // 2
TAIL = 8
D_IN_PAD = 2 * A_WIDTH + 4 * B_WIDTH + 2 * C_WIDTH + 2 * C_KEY + GATE_W
VMEM_LIMIT = 56 * 1024 * 1024

D_H = D_IN_PAD - 2 * A_WIDTH

COL_BQ, COL_BK, COL_BV, COL_BZ = 0, 1, 2, 3
COL_CV, COL_CR = 8, 9
COL_CQ, COL_CK, COL_G = 20, 21, 22


def _dot(a, b, **kw):
    return jnp.dot(a, b, preferred_element_type=F32, **kw)


def _dot_nt(a, b, **kw):
    return lax.dot_general(a, b, (((1,), (1,)), ((), ())), preferred_element_type=F32, **kw)


def _dot_tn(a, b, **kw):
    return lax.dot_general(a, b, (((0,), (0,)), ((), ())), preferred_element_type=F32, **kw)


def _split2(x):
    hi = x.astype(BF16)
    lo = (x - hi.astype(F32)).astype(BF16)
    return jnp.concatenate([hi, lo], axis=-1)


def _split3(x, axis):
    hi = x.astype(BF16)
    r1 = x - hi.astype(F32)
    mid = r1.astype(BF16)
    lo = (r1 - mid.astype(F32)).astype(BF16)
    return jnp.concatenate([hi, mid, lo], axis=axis)


def _iota(shape, dim):
    return lax.broadcasted_iota(jnp.int32, shape, dim)


def _group_ones(width, group):
    r = _iota((width, width), 0)
    c = _iota((width, width), 1)
    return jnp.where((r // group) == (c // group), 1.0, 0.0).astype(BF16)


def _group_sumsq(x, ones):
    return _dot((x * x).astype(BF16), ones)


def _layer_norm(y, g, b):
    mu = jnp.mean(y, axis=-1, keepdims=True)
    yc = y - mu
    var = jnp.mean(yc * yc, axis=-1, keepdims=True)
    return yc * lax.rsqrt(var + EPS) * g + b


def _silu(x):
    return x * jax.nn.sigmoid(x)


def _gelu_tanh(x):
    return 0.5 * x * (1.0 + jnp.tanh(0.7978845608028654 * (x + 0.044715 * (x * x * x))))


def _softplus(x):
    return jnp.maximum(x, 0.0) + jnp.log(1.0 + jnp.exp(-jnp.abs(x)))


FFN_TM = 512
FFN_TF = 256
FFN_SLABS = 2


def _ffn_ln_rows(x, rows, w1_ref, w3_ref, w2_ref, g_ref, b_ref, a_ref):
    xb = x.astype(BF16)
    for j in range(D_FF // FFN_TF):
        sl = slice(j * FFN_TF, (j + 1) * FFN_TF)
        h1 = _dot(xb, w1_ref[:, sl])
        h3 = _dot(xb, w3_ref[:, sl])
        a_ref[rows, sl] = (_silu(h1) * h3).astype(BF16)
    y = ALPHA * x + 0.5 * _dot(a_ref[rows, :], w2_ref[...])
    return _layer_norm(y, g_ref[...], b_ref[...])


def _row_slabs(tm):
    step = tm // FFN_SLABS
    return [slice(r * step, (r + 1) * step) for r in range(FFN_SLABS)]


def _ffn_ln_kernel(x_ref, w1_ref, w3_ref, w2_ref, g_ref, b_ref, o_ref, a_ref):
    for rows in _row_slabs(x_ref.shape[0]):
        o_ref[rows, :] = _ffn_ln_rows(x_ref[rows, :], rows, w1_ref, w3_ref, w2_ref, g_ref, b_ref, a_ref)


def _layer_spec(shape, l, **kw):
    zeros = (0,) * len(shape)
    return pl.BlockSpec((None,) + tuple(shape), lambda *_: (l,) + zeros, **kw)


def _ffn_ln(x, w1, w3, w2, g, b, l):
    n = x.shape[0]
    tm = min(FFN_TM, n)
    return pl.pallas_call(
        _ffn_ln_kernel,
        grid=(n // tm,),
        in_specs=[
            pl.BlockSpec((tm, D_MODEL), lambda i: (i, 0)),
            _layer_spec((D_MODEL, D_FF), l, pipeline_mode=pl.Buffered(1)),
            _layer_spec((D_MODEL, D_FF), l, pipeline_mode=pl.Buffered(1)),
            _layer_spec((D_FF, D_MODEL), l, pipeline_mode=pl.Buffered(1)),
            _layer_spec((1, D_MODEL), l),
            _layer_spec((1, D_MODEL), l),
        ],
        out_specs=pl.BlockSpec((tm, D_MODEL), lambda i: (i, 0)),
        out_shape=jax.ShapeDtypeStruct((n, D_MODEL), F32),
        scratch_shapes=[pltpu.VMEM((tm, D_FF), BF16)],
        compiler_params=pltpu.CompilerParams(dimension_semantics=("arbitrary",), vmem_limit_bytes=VMEM_LIMIT),
        name="ffn_ln",
    )(x, w1, w3, w2, g, b)


PROJ_TM = 512


def _proj_sgu_kernel(tiles_per_seq, x_ref, w_ref, lng_ref, lnb_ref, sw_ref, bias_ref, cw_ref, h_ref, a_ref, xbuf):
    tm = x_ref.shape[0]
    w3 = 3 * B_WIDTH
    c0 = 2 * A_WIDTH

    @pl.when(pl.program_id(0) % tiles_per_seq == 0)
    def _():
        xbuf[0:TAIL, :] = jnp.zeros((TAIL, w3), F32)

    xb = x_ref[...].astype(BF16)
    xbuf[TAIL:TAIL + tm, :] = _dot(xb, w_ref[:, c0:c0 + w3])
    y = cw_ref[B_CONV - 1:B_CONV, :] * xbuf[TAIL:TAIL + tm, :]
    for i in range(B_CONV - 1):
        back = B_CONV - 1 - i
        y = y + cw_ref[i:i + 1, :] * xbuf[TAIL - back:TAIL - back + tm, :]
    xbuf[0:TAIL, :] = xbuf[tm:tm + TAIL, :]
    y = _silu(y)
    h_ref[:, 2 * B_WIDTH:w3] = y[:, 2 * B_WIDTH:w3]

    h_ref[:, w3:] = _dot(xb, w_ref[:, c0 + w3:])
    ha = _dot(xb, w_ref[:, 0:c0])
    u = _gelu_tanh(ha[:, 0:A_WIDTH])
    v = _layer_norm(_gelu_tanh(ha[:, A_WIDTH:2 * A_WIDTH]), lng_ref[...], lnb_ref[...])
    tril = _iota((A_CHUNK, A_CHUNK), 1) <= _iota((A_CHUNK, A_CHUNK), 0)
    head_of_lane = _iota((A_CHUNK, A_WIDTH), 1) // HEAD_DIM
    wts = [jnp.where(tril, sw_ref[h], 0.0).astype(BF16) for h in range(A_HEADS)]
    bias = bias_ref[...]
    for c in range(tm // A_CHUNK):
        rows = slice(c * A_CHUNK, (c + 1) * A_CHUNK)
        vc = v[rows]
        z = bias
        for h in range(A_HEADS):
            z = z + _dot(wts[h], jnp.where(head_of_lane == h, vc, 0.0).astype(BF16))
        a_ref[rows, :] = (u[rows] * z).astype(a_ref.dtype)

    ones2 = _group_ones(LANES, B_DK)
    for p in range(N_PAIRS):
        sl = slice(p * LANES, (p + 1) * LANES)
        yq = y[:, sl]
        h_ref[:, sl] = yq * (lax.rsqrt(_group_sumsq(yq, ones2) + 1e-6) * (B_DK ** -0.5))
        ksl = slice(B_WIDTH + p * LANES, B_WIDTH + (p + 1) * LANES)
        yk = y[:, ksl]
        h_ref[:, ksl] = yk * lax.rsqrt(_group_sumsq(yk, ones2) + 1e-6)


def _proj_sgu(x, w, lng, lnb, sw, bias, cw, seq, l):
    n = x.shape[0]
    tm = min(PROJ_TM, seq)
    return pl.pallas_call(
        functools.partial(_proj_sgu_kernel, seq // tm),
        grid=(n // tm,),
        in_specs=[
            pl.BlockSpec((tm, D_MODEL), lambda i: (i, 0)),
            _layer_spec((D_MODEL, D_IN_PAD), l, pipeline_mode=pl.Buffered(1)),
            _layer_spec((1, A_WIDTH), l),
            _layer_spec((1, A_WIDTH), l),
            _layer_spec((A_HEADS, A_CHUNK, A_CHUNK), l),
            _layer_spec((A_CHUNK, A_WIDTH), l),
            _layer_spec((B_CONV, 3 * B_WIDTH), l),
        ],
        out_specs=[pl.BlockSpec((tm, D_H), lambda i: (i, 0)), pl.BlockSpec((tm, A_WIDTH), lambda i: (i, 0))],
        out_shape=[jax.ShapeDtypeStruct((n, D_H), F32), jax.ShapeDtypeStruct((n, A_WIDTH), BF16)],
        scratch_shapes=[pltpu.VMEM((tm + TAIL, 3 * B_WIDTH), F32)],
        compiler_params=pltpu.CompilerParams(dimension_semantics=("arbitrary",), vmem_limit_bytes=VMEM_LIMIT),
        name="proj_sgu",
    )(x, w, lng, lnb, sw, bias, cw)


GLA_TB = 256


def _gla_kernel(q_ref, k_ref, v_ref, r_ref, g_ref, gup_ref, gb_ref, tri_ref, ng_ref, o_ref, st_ref, ob_ref):
    nb, tb = q_ref.shape[0], q_ref.shape[1]
    nc = tb // CHUNK

    @pl.when(pl.program_id(0) == 0)
    def _():
        st_ref[...] = jnp.zeros_like(st_ref)

    khead = _iota((CHUNK, C_KEY), 1) // C_DK
    vhead = _iota((CHUNK, C_WIDTH), 1) // C_DV
    causal = (_iota((CHUNK, C_WIDTH), 1) % CHUNK) <= _iota((CHUNK, C_WIDTH), 0)
    st_mask = (_iota((C_WIDTH, C_KEY), 0) // C_DV) == (_iota((C_WIDTH, C_KEY), 1) // C_DK)

    units = []
    for b in range(nb):
        logit = _dot(g_ref[b], gup_ref[...], precision=HIGHEST) + gb_ref[...]
        log_a = -_softplus(-logit) * (1.0 / C_TAU)
        g3 = _dot(tri_ref[...], _split3(log_a, 1))
        bc_all = g3[:, 0:C_KEY] + g3[:, C_KEY:2 * C_KEY] + g3[:, 2 * C_KEY:3 * C_KEY]
        for c in range(nc):
            rows = slice(c * CHUNK, (c + 1) * CHUNK)
            bc = bc_all[rows]
            btot = bc[CHUNK - 1:CHUNK, :]
            k = k_ref[b, rows, :]
            kn = k * jnp.exp(-bc)
            vb = v_ref[b, rows, :].astype(BF16)
            units.append(dict(
                b=b, rows=rows, vb=vb, gt=jnp.exp(btot),
                qd=(q_ref[b, rows, :] * ((C_DK ** -0.5) * jnp.exp(bc))).astype(BF16),
                kd=(k * jnp.exp(btot - bc)).astype(BF16),
                kstack=jnp.concatenate([jnp.where(khead == h, kn, 0.0) for h in range(C_HEADS)], axis=0).astype(BF16),
                vbd=jnp.concatenate([jnp.where(vhead == h, vb, jnp.zeros_like(vb)) for h in range(C_HEADS)], axis=0)))
    for u in units:
        u["p"] = jnp.where(causal, _dot_nt(u["qd"], u["kstack"]), 0.0).astype(BF16)
    for u in units:
        u["upd"] = jnp.where(st_mask, _dot_tn(u["vb"], u["kd"]), 0.0)
    for u in units:
        u["oi"] = _dot(u["p"], u["vbd"])

    states = [st_ref[b] for b in range(nb)]
    for u in units:
        b = u["b"]
        ob_ref[b, u["rows"], :] = u["oi"] + _dot_nt(u["qd"], states[b].astype(BF16))
        states[b] = u["gt"] * states[b] + u["upd"]
    for b in range(nb):
        st_ref[b] = states[b]

    ones = _group_ones(C_WIDTH, C_DV)
    for b in range(nb):
        o = ob_ref[b]
        ss = _group_sumsq(o, ones)
        o_ref[b] = (o * lax.rsqrt(ss * (1.0 / C_DV) + EPS) * ng_ref[...] * _silu(r_ref[b])).astype(o_ref.dtype)


def _gla(h3, gup, gb, ng, l):
    batch, seq, _ = h3.shape
    tb = min(GLA_TB, seq)
    hspec = lambda width, col: pl.BlockSpec((batch, tb, width), lambda t: (0, t, col))
    return pl.pallas_call(
        _gla_kernel,
        grid=(seq // tb,),
        in_specs=[
            hspec(C_KEY, COL_CQ), hspec(C_KEY, COL_CK), hspec(C_WIDTH, COL_CV), hspec(C_WIDTH, COL_CR),
            hspec(GATE_W, COL_G),
            _layer_spec((GATE_W, C_KEY), l),
            _layer_spec((1, C_KEY), l),
            pl.BlockSpec((tb, tb), lambda t: (0, 0)),
            _layer_spec((1, C_WIDTH), l),
        ],
        out_specs=pl.BlockSpec((batch, tb, C_WIDTH), lambda t: (0, t, 0)),
        out_shape=jax.ShapeDtypeStruct((batch, seq, C_WIDTH), BF16),
        scratch_shapes=[pltpu.VMEM((batch, C_WIDTH, C_KEY), F32), pltpu.VMEM((batch, tb, C_WIDTH), F32)],
        compiler_params=pltpu.CompilerParams(dimension_semantics=("arbitrary",), vmem_limit_bytes=VMEM_LIMIT),
        name="gla",
    )(h3, h3, h3, h3, h3, gup, gb, _chunk_triangles(tb)[0], ng)


GDN_TB = 256
GROUP_CHUNKS = 4


def _gdn_kernel(q_ref, k_ref, v_ref, z_ref, g_ref, arow_ref, dtrow_ref, ebg_ref, tri_ref, ng_ref, o_ref,
                bg_s, rows_s, qa_s, bb_s, gt_s, o_s, s_ref):
    nb, tb = q_ref.shape[0], q_ref.shape[1]
    nc = tb // CHUNK

    @pl.when(pl.program_id(0) == 0)
    def _():
        s_ref[...] = jnp.zeros_like(s_ref)

    ones2 = _group_ones(LANES, B_DK)
    lane = _iota((tb, GATE_W), 1)
    low_half = _iota((1, LANES), 1) < CHUNK
    for b in range(nb):
        gates = g_ref[b]
        beta = jax.nn.sigmoid(gates)
        lg = jnp.where((lane >= B_HEADS) & (lane < 2 * B_HEADS),
                       -jnp.exp(arow_ref[...]) * _softplus(gates + dtrow_ref[...]), 0.0)
        g3 = _dot(tri_ref[0], _split3(lg, 1))
        gam = g3[:, 0:GATE_W] + g3[:, GATE_W:2 * GATE_W] + g3[:, 2 * GATE_W:3 * GATE_W]
        feat = jnp.where(lane < B_HEADS, beta, gam)
        bg_s[b] = _dot(_split2(feat), ebg_ref[...])

        lg_t3 = _split3(lg.T[0:2 * B_HEADS, :], 0)
        nh2 = 2 * B_HEADS
        t3 = _dot(lg_t3, tri_ref[1])
        gam_t = t3[0:nh2] + t3[nh2:2 * nh2] + t3[2 * nh2:3 * nh2]
        t3 = _dot(lg_t3, tri_ref[2])
        gam_t_sw = t3[0:nh2] + t3[nh2:2 * nh2] + t3[2 * nh2:3 * nh2]
        for c in range(nc):
            tile = slice((c // 2) * LANES, (c // 2 + 1) * LANES)
            first, second = (gam_t, gam_t_sw) if c % 2 == 0 else (gam_t_sw, gam_t)
            for p in range(N_PAIRS):
                h0 = B_HEADS + 2 * p
                rows_s[b, c, p:p + 1, :] = jnp.where(low_half, first[h0:h0 + 1, tile], second[h0 + 1:h0 + 2, tile])

    ii = _iota((CHUNK, LANES), 0)
    jj = _iota((CHUNK, LANES), 1) % CHUNK
    incl = jj <= ii
    strict = jj < ii
    eye = jnp.where(jj == ii, 1.0, 0.0)
    lo_lane = _iota((CHUNK, LANES), 1) < CHUNK
    bd_mask = (_iota((LANES, LANES), 0) // CHUNK) == (_iota((LANES, LANES), 1) // CHUNK)

    def blockdiag(m):
        zero = jnp.zeros_like(m)
        return jnp.concatenate([jnp.where(lo_lane, m, zero), jnp.where(lo_lane, zero, m)], axis=0)

    groups_per_batch = nc // GROUP_CHUNKS

    def solve_group(g, carry):
        b = g // groups_per_batch
        c0 = (g % groups_per_batch) * GROUP_CHUNKS
        units = []
        for dc in range(GROUP_CHUNKS):
            c = c0 + dc
            rws = pl.ds(pl.multiple_of(c * CHUNK, CHUNK), CHUNK)
            grows = rows_s[b, c]
            for p in range(N_PAIRS):
                sl = slice(p * LANES, (p + 1) * LANES)
                q = q_ref[b, rws, sl]
                k = k_ref[b, rws, sl]
                v = v_ref[b, rws, sl]
                bet = bg_s[b, rws, sl]
                gi = bg_s[b, rws, B_WIDTH + p * LANES:B_WIDTH + (p + 1) * LANES]
                dec = jnp.exp(jnp.minimum(gi - grows[p:p + 1, :], 0.0))
                eg = jnp.exp(gi)
                glast = gi[CHUNK - 1:CHUNK, :]
                k16 = k.astype(BF16)
                qk = _dot_nt(jnp.concatenate([q.astype(BF16), k16], axis=0), blockdiag(k16))
                nk = jnp.where(strict, qk[CHUNK:2 * CHUNK] * (-bet) * dec, 0.0)
                gt_s[b, c, p:p + 1, :] = jnp.exp(glast)
                units.append(dict(
                    c=c, p=p, rws=rws, sl=sl,
                    pm=jnp.where(incl, qk[0:CHUNK] * dec, 0.0).astype(BF16),
                    tm=eye + nk, nk=nk.astype(BF16),
                    rhs=jnp.concatenate([blockdiag((v * bet).astype(BF16)),
                                         blockdiag((k * (bet * eg)).astype(BF16))], axis=1),
                    qd=q * eg, kd=(k * jnp.exp(glast - gi)).astype(BF16)))
        for u in units:
            u["nk"] = _dot(u["nk"], blockdiag(u["nk"])).astype(BF16)
        for _ in range(4):
            for u in units:
                both = _dot(u["nk"], jnp.concatenate([blockdiag(u["nk"]), blockdiag(u["tm"].astype(BF16))], axis=1))
                u["nk"] = both[:, 0:LANES].astype(BF16)
                u["tm"] = u["tm"] + both[:, LANES:2 * LANES]
        for u in units:
            u["tm"] = u["tm"] + _dot(u["nk"], blockdiag(u["tm"].astype(BF16)))
        for u in units:
            u["sol"] = _dot(u["tm"].astype(BF16), u["rhs"]).astype(BF16)
        for u in units:
            ab = _dot_tn(u["kd"], u["sol"])
            bb_s[b, u["c"], u["p"]] = jnp.where(bd_mask, ab[:, 0:LANES], 0.0)
            qa_s[b, u["c"], u["p"], CHUNK:CHUNK + LANES, :] = jnp.where(bd_mask, -ab[:, LANES:2 * LANES], 0.0).astype(BF16)
        for u in units:
            sol = u["sol"]
            po = _dot(u["pm"], jnp.concatenate([blockdiag(sol[:, 0:LANES]), blockdiag(sol[:, LANES:2 * LANES])], axis=1))
            o_s[b, u["rws"], u["sl"]] = po[:, 0:LANES]
            qa_s[b, u["c"], u["p"], 0:CHUNK, :] = (u["qd"] - po[:, LANES:2 * LANES]).astype(BF16)
        return carry

    lax.fori_loop(0, nb * groups_per_batch, solve_group, 0)

    def scan_chunk(c, carry):
        rws = pl.ds(pl.multiple_of(c * CHUNK, CHUNK), CHUNK)
        chains = [(b, p) for b in range(nb) for p in range(N_PAIRS)]
        states = [s_ref[b, p] for b, p in chains]
        prods = [_dot(qa_s[b, c, p], s.astype(BF16)) for (b, p), s in zip(chains, states)]
        for (b, p), s, r in zip(chains, states, prods):
            sl = slice(p * LANES, (p + 1) * LANES)
            s_ref[b, p] = gt_s[b, c, p:p + 1, :] * s + r[CHUNK:CHUNK + LANES] + bb_s[b, c, p]
            o_s[b, rws, sl] = o_s[b, rws, sl] + r[0:CHUNK]
        return carry

    lax.fori_loop(0, nc, scan_chunk, 0)

    for b in range(nb):
        for p in range(N_PAIRS):
            sl = slice(p * LANES, (p + 1) * LANES)
            o = o_s[b, :, sl]
            ss = _group_sumsq(o, ones2)
            o_ref[b, :, sl] = (o * lax.rsqrt(ss * (1.0 / 64.0) + EPS) * ng_ref[...]
                               * _silu(z_ref[b, :, sl])).astype(o_ref.dtype)


def _chunk_triangles(tb):
    r = np.arange(tb)[:, None]
    c = np.arange(tb)[None, :]
    same = (r // CHUNK) == (c // CHUNK)
    c_sw = np.where((c % LANES) < CHUNK, c + CHUNK, c - CHUNK)
    tril = same & (c <= r)
    triu = same & (r <= c)
    triu_sw = ((r // CHUNK) == (c_sw // CHUNK)) & (r <= c_sw)
    return jnp.asarray(np.stack([tril, triu, triu_sw]).astype(np.float32), BF16)


def _gdn(h3, arow, dtrow, ebg, ng, l):
    batch, seq, _ = h3.shape
    tb = min(GDN_TB, seq)
    nc = tb // CHUNK
    hspec = lambda width, col: pl.BlockSpec((batch, tb, width), lambda t: (0, t, col))
    return pl.pallas_call(
        _gdn_kernel,
        grid=(seq // tb,),
        in_specs=[
            hspec(B_WIDTH, COL_BQ), hspec(B_WIDTH, COL_BK), hspec(B_WIDTH, COL_BV), hspec(B_WIDTH, COL_BZ),
            hspec(GATE_W, COL_G),
            _layer_spec((1, GATE_W), l),
            _layer_spec((1, GATE_W), l),
            pl.BlockSpec((2 * GATE_W, 2 * B_WIDTH), lambda t: (0, 0)),
            pl.BlockSpec((3, tb, tb), lambda t: (0, 0, 0)),
            _layer_spec((1, LANES), l),
        ],
        out_specs=pl.BlockSpec((batch, tb, B_WIDTH), lambda t: (0, t, 0)),
        out_shape=jax.ShapeDtypeStruct((batch, seq, B_WIDTH), BF16),
        scratch_shapes=[
            pltpu.VMEM((batch, tb, 2 * B_WIDTH), F32),
            pltpu.VMEM((batch, nc, 8, LANES), F32),
            pltpu.VMEM((batch, nc, N_PAIRS, CHUNK + LANES, LANES), BF16),
            pltpu.VMEM((batch, nc, N_PAIRS, LANES, LANES), F32),
            pltpu.VMEM((batch, nc, 8, LANES), F32),
            pltpu.VMEM((batch, tb, B_WIDTH), F32),
            pltpu.VMEM((batch, N_PAIRS, LANES, LANES), F32),
        ],
        compiler_params=pltpu.CompilerParams(dimension_semantics=("arbitrary",), vmem_limit_bytes=VMEM_LIMIT),
        name="gdn",
    )(h3, h3, h3, h3, h3, arow, dtrow, ebg, _chunk_triangles(tb), ng)


def _out_ffn_kernel(x_ref, a_ref, b_ref, c_ref, wo_ref, g2_ref, b2_ref, w1_ref, w3_ref, w2_ref, g3_ref, b3_ref,
                    o_ref, act_ref):
    slabs = _row_slabs(x_ref.shape[0])
    mixes = []
    for rows in slabs:
        mix = _dot(a_ref[rows, :], wo_ref[0:A_WIDTH, :])
        mix = mix + _dot(b_ref[rows, :], wo_ref[A_WIDTH:A_WIDTH + B_WIDTH, :])
        mixes.append(mix + _dot(c_ref[rows, :], wo_ref[A_WIDTH + B_WIDTH:, :]))
    for rows, mix in zip(slabs, mixes):
        x2 = _layer_norm(ALPHA * x_ref[rows, :] + mix, g2_ref[...], b2_ref[...])
        o_ref[rows, :] = _ffn_ln_rows(x2, rows, w1_ref, w3_ref, w2_ref, g3_ref, b3_ref, act_ref)


def _out_ffn(x, a, b, c, wo, g2, b2, w1, w3, w2, g3, b3, l):
    n = x.shape[0]
    tm = min(FFN_TM, n)
    rows = lambda width: pl.BlockSpec((tm, width), lambda i: (i, 0))
    return pl.pallas_call(
        _out_ffn_kernel,
        grid=(n // tm,),
        in_specs=[
            rows(D_MODEL), rows(A_WIDTH), rows(B_WIDTH), rows(C_WIDTH),
            _layer_spec((D_MODEL, D_MODEL), l, pipeline_mode=pl.Buffered(1)),
            _layer_spec((1, D_MODEL), l),
            _layer_spec((1, D_MODEL), l),
            _layer_spec((D_MODEL, D_FF), l, pipeline_mode=pl.Buffered(1)),
            _layer_spec((D_MODEL, D_FF), l, pipeline_mode=pl.Buffered(1)),
            _layer_spec((D_FF, D_MODEL), l, pipeline_mode=pl.Buffered(1)),
            _layer_spec((1, D_MODEL), l),
            _layer_spec((1, D_MODEL), l),
        ],
        out_specs=rows(D_MODEL),
        out_shape=jax.ShapeDtypeStruct((n, D_MODEL), F32),
        scratch_shapes=[pltpu.VMEM((tm, D_FF), BF16)],
        compiler_params=pltpu.CompilerParams(dimension_semantics=("arbitrary",), vmem_limit_bytes=VMEM_LIMIT),
        name="out_ffn",
    )(x, a, b, c, wo, g2, b2, w1, w3, w2, g3, b3)


def _pad_w_in(w_in):
    o = np.cumsum([0, A_WIDTH, A_WIDTH, 3 * B_WIDTH, B_WIDTH, B_HEADS, B_HEADS, C_KEY, C_KEY, C_WIDTH, C_WIDTH, C_RANK])
    a_u, a_v, b_qkv, b_z, b_beta, b_dec, c_q, c_k, c_v, c_r, c_g = (w_in[..., o[i]:o[i + 1]] for i in range(11))
    pad = jnp.zeros(w_in.shape[:-1] + (GATE_W - 2 * B_HEADS - C_RANK,), w_in.dtype)
    return jnp.concatenate([a_u, a_v, b_qkv, b_z, c_v, c_r, c_q, c_k, b_beta, b_dec, c_g, pad], axis=-1).astype(BF16)


def _bcast_selector():
    e = np.zeros((2 * GATE_W, 2 * B_WIDTH), np.float32)
    for j in range(B_WIDTH):
        hd = j // 64
        for base in (0, GATE_W):
            e[base + hd, j] = 1.0
            e[base + B_HEADS + hd, B_WIDTH + j] = 1.0
    return jnp.asarray(e, BF16)


def _gate_row(vec, offset):
    l, n = vec.shape
    return jnp.zeros((l, 1, GATE_W), F32).at[:, 0, offset:offset + n].set(vec)


def kernel(x, ffn1_w1, ffn1_w3, ffn1_w2, ln1_g, ln1_b, w_in, sgu_ln_g, sgu_ln_b, sgu_w, sgu_b, gdn_conv_w, gdn_a_log, gdn_dt_bias, gdn_norm_g, gla_gate_up, gla_gate_b, gla_norm_g, w_out, ln2_g, ln2_b, ffn2_w1, ffn2_w3, ffn2_w2, ln3_g, ln3_b):
    batch, seq, d = x.shape
    depth = w_in.shape[0]
    row = lambda p: p[:, None, :]
    f1 = [w.astype(BF16) for w in (ffn1_w1, ffn1_w3, ffn1_w2)]
    f2 = [w.astype(BF16) for w in (ffn2_w1, ffn2_w3, ffn2_w2)]
    w_in_p = _pad_w_in(w_in)
    w_out_b = w_out.astype(BF16)
    sgu_bias = jnp.repeat(jnp.swapaxes(sgu_b, 1, 2), HEAD_DIM, axis=2)
    arow = _gate_row(gdn_a_log, B_HEADS)
    dtrow = _gate_row(gdn_dt_bias, B_HEADS)
    ebg = _bcast_selector()
    gdn_ng = jnp.tile(gdn_norm_g, (1, 2))[:, None, :]
    gla_ng = jnp.tile(gla_norm_g, (1, C_HEADS))[:, None, :]
    gup = jnp.zeros((depth, GATE_W, C_KEY), F32).at[:, 2 * B_HEADS:2 * B_HEADS + C_RANK, :].set(gla_gate_up)

    xf = x.reshape(batch * seq, d)
    for l in range(depth):
        xf = _ffn_ln(xf, *f1, row(ln1_g), row(ln1_b), l)
        h, out_a = _proj_sgu(xf, w_in_p, row(sgu_ln_g), row(sgu_ln_b), sgu_w, sgu_bias, gdn_conv_w, seq, l)
        h3 = h.reshape(batch, seq, D_H)
        out_b = _gdn(h3, arow, dtrow, ebg, gdn_ng, l).reshape(batch * seq, B_WIDTH)
        out_c = _gla(h3, gup, row(gla_gate_b), gla_ng, l).reshape(batch * seq, C_WIDTH)
        xf = _out_ffn(xf, out_a, out_b, out_c, w_out_b, row(ln2_g), row(ln2_b), *f2, row(ln3_g), row(ln3_b), l)
    return xf.reshape(batch, seq, d)
```

```python
import functools

import numpy as np
import jax
import jax.numpy as jnp
from jax import lax
from jax.experimental import pallas as pl
from jax.experimental.pallas import tpu as pltpu

F32 = jnp.float32
BF16 = jnp.bfloat16
HIGHEST = lax.Precision.HIGHEST

D_MODEL = 1024
DEPTH = 4
HEAD_DIM = 64
A_HEADS = 4
A_WIDTH = A_HEADS * HEAD_DIM
A_CHUNK = 128
B_HEADS = 8
B_DK = 64
B_WIDTH = B_HEADS * 64
B_CONV = 4
C_HEADS = 4
C_DK = 32
C_DV = 64
C_KEY = C_HEADS * C_DK
C_WIDTH = C_HEADS * C_DV
C_RANK = 16
C_TAU = 16.0
D_FF = 2816
ALPHA = (2.0 * DEPTH) ** 0.25
EPS = 1e-5

LANES = 128
CHUNK = 64
GATE_W = LANES
N_PAIRS = B_HEADS // 2
TAIL = 8
D_IN_PAD = 2 * A_WIDTH + 4 * B_WIDTH + 2 * C_WIDTH + 2 * C_KEY + GATE_W
VMEM_LIMIT = 56 * 1024 * 1024

D_H = D_IN_PAD - 2 * A_WIDTH

COL_BQ, COL_BK, COL_BV, COL_BZ = 0, 1, 2, 3
COL_CV, COL_CR = 8, 9
COL_CQ, COL_CK, COL_G = 20, 21, 22


def _dot(a, b, **kw):
    return jnp.dot(a, b, preferred_element_type=F32, **kw)


def _dot_nt(a, b, **kw):
    return lax.dot_general(a, b, (((1,), (1,)), ((), ())), preferred_element_type=F32, **kw)


def _dot_tn(a, b, **kw):
    return lax.dot_general(a, b, (((0,), (0,)), ((), ())), preferred_element_type=F32, **kw)


def _split2(x):
    hi = x.astype(BF16)
    lo = (x - hi.astype(F32)).astype(BF16)
    return jnp.concatenate([hi, lo], axis=-1)


def _split3(x, axis):
    hi = x.astype(BF16)
    r1 = x - hi.astype(F32)
    mid = r1.astype(BF16)
    lo = (r1 - mid.astype(F32)).astype(BF16)
    return jnp.concatenate([hi, mid, lo], axis=axis)


def _iota(shape, dim):
    return lax.broadcasted_iota(jnp.int32, shape, dim)


def _group_ones(width, group):
    r = _iota((width, width), 0)
    c = _iota((width, width), 1)
    return jnp.where((r // group) == (c // group), 1.0, 0.0).astype(BF16)


def _group_sumsq(x, ones):
    return _dot((x * x).astype(BF16), ones)


def _layer_norm(y, g, b):
    mu = jnp.mean(y, axis=-1, keepdims=True)
    yc = y - mu
    var = jnp.mean(yc * yc, axis=-1, keepdims=True)
    return yc * lax.rsqrt(var + EPS) * g + b


def _silu(x):
    return x * jax.nn.sigmoid(x)


def _gelu_tanh(x):
    return 0.5 * x * (1.0 + jnp.tanh(0.7978845608028654 * (x + 0.044715 * (x * x * x))))


def _softplus(x):
    return jnp.maximum(x, 0.0) + jnp.log(1.0 + jnp.exp(-jnp.abs(x)))


FFN_TM = 512
FFN_TF = 256
FFN_SLABS = 2


def _ffn_ln_rows(x, rows, w1_ref, w3_ref, w2_ref, g_ref, b_ref, a_ref):
    xb = x.astype(BF16)
    for j in range(D_FF // FFN_TF):
        sl = slice(j * FFN_TF, (j + 1) * FFN_TF)
        h1 = _dot(xb, w1_ref[:, sl])
        h3 = _dot(xb, w3_ref[:, sl])
        a_ref[rows, sl] = (_silu(h1) * h3).astype(BF16)
    y = ALPHA * x + 0.5 * _dot(a_ref[rows, :], w2_ref[...])
    return _layer_norm(y, g_ref[...], b_ref[...])


def _row_slabs(tm):
    step = tm // FFN_SLABS
    return [slice(r * step, (r + 1) * step) for r in range(FFN_SLABS)]


def _stream_in(plan, sem):
    used = {}
    steps = []
    for src, stage, sid, consume in plan:
        slot = used.get(sid, 0) % 2
        used[sid] = used.get(sid, 0) + 1
        steps.append((pltpu.make_async_copy(src, stage.at[slot], sem.at[sid, slot]), stage, slot, consume))
    steps[0][0].start()
    for k, (copy, stage, slot, consume) in enumerate(steps):
        if k + 1 < len(steps):
            steps[k + 1][0].start()
        copy.wait()
        consume(stage[slot])


def _store_cast(dst, rows, cols):
    def consume(v):
        dst[rows, cols] = v.astype(BF16)
    return consume


def _ffn_weight_plan(l, w1_hbm, w3_hbm, w2_hbm, w1b, w3b, w2b, col_stage, row_stage):
    plan = []
    full = slice(None)
    for j in range(D_FF // FFN_TF):
        win = pl.ds(j * FFN_TF, FFN_TF)
        sl = slice(j * FFN_TF, (j + 1) * FFN_TF)
        plan.append((w1_hbm.at[l, :, win], col_stage, 0, _store_cast(w1b, full, sl)))
        plan.append((w3_hbm.at[l, :, win], col_stage, 0, _store_cast(w3b, full, sl)))
    for j in range(D_FF // FFN_TF):
        plan.append((w2_hbm.at[l, pl.ds(j * FFN_TF, FFN_TF), :], row_stage, 1,
                     _store_cast(w2b, slice(j * FFN_TF, (j + 1) * FFN_TF), full)))
    return plan


def _ffn_scratch(tm):
    return [
        pltpu.VMEM((tm, D_FF), BF16),
        pltpu.VMEM((D_MODEL, D_FF), BF16),
        pltpu.VMEM((D_MODEL, D_FF), BF16),
        pltpu.VMEM((D_FF, D_MODEL), BF16),
        pltpu.VMEM((2, D_MODEL, FFN_TF), F32),
        pltpu.VMEM((2, FFN_TF, D_MODEL), F32),
    ]


def _ffn_ln_kernel(l, x_ref, w1_hbm, w3_hbm, w2_hbm, g_ref, b_ref, o_ref,
                   a_ref, w1b, w3b, w2b, col_stage, row_stage, sem):
    @pl.when(pl.program_id(0) == 0)
    def _():
        _stream_in(_ffn_weight_plan(l, w1_hbm, w3_hbm, w2_hbm, w1b, w3b, w2b, col_stage, row_stage), sem)

    for rows in _row_slabs(x_ref.shape[0]):
        o_ref[rows, :] = _ffn_ln_rows(x_ref[rows, :], rows, w1b, w3b, w2b, g_ref, b_ref, a_ref)


def _layer_spec(shape, l, **kw):
    zeros = (0,) * len(shape)
    return pl.BlockSpec((None,) + tuple(shape), lambda *_: (l,) + zeros, **kw)


HBM = pl.BlockSpec(memory_space=pl.ANY)


def _ffn_ln(x, w1, w3, w2, g, b, l):
    n = x.shape[0]
    tm = min(FFN_TM, n)
    return pl.pallas_call(
        functools.partial(_ffn_ln_kernel, l),
        grid=(n // tm,),
        in_specs=[
            pl.BlockSpec((tm, D_MODEL), lambda i: (i, 0)),
            HBM, HBM, HBM,
            _layer_spec((1, D_MODEL), l),
            _layer_spec((1, D_MODEL), l),
        ],
        out_specs=pl.BlockSpec((tm, D_MODEL), lambda i: (i, 0)),
        out_shape=jax.ShapeDtypeStruct((n, D_MODEL), F32),
        scratch_shapes=_ffn_scratch(tm) + [pltpu.SemaphoreType.DMA((2, 2))],
        compiler_params=pltpu.CompilerParams(dimension_semantics=("arbitrary",), vmem_limit_bytes=VMEM_LIMIT),
        name="ffn_ln",
    )(x, w1, w3, w2, g, b)


PROJ_TM = 512
PROJ_WR = 128
D_IN = 2 * A_WIDTH + 4 * B_WIDTH + 2 * B_HEADS + 2 * C_KEY + 2 * C_WIDTH + C_RANK


def _permute_w_in(v):
    o = np.cumsum([0, A_WIDTH, A_WIDTH, 3 * B_WIDTH, B_WIDTH, B_HEADS, B_HEADS, C_KEY, C_KEY, C_WIDTH, C_WIDTH, C_RANK])
    front = v[:, 0:o[4]]
    gates_b = v[:, o[4]:o[6]]
    c_qk = v[:, o[6]:o[8]]
    c_vr = v[:, o[8]:o[10]]
    c_g = v[:, o[10]:o[11]]
    pad = jnp.zeros((v.shape[0], GATE_W - 2 * B_HEADS - C_RANK), F32)
    return jnp.concatenate([front, c_vr, c_qk, gates_b, c_g, pad], axis=-1).astype(BF16)


def _proj_sgu_kernel(tiles_per_seq, l, x_ref, w_hbm, lng_ref, lnb_ref, sw_ref, bias_ref, cw_ref, h_ref, a_ref,
                     xbuf, w_ref, w_stage, sem):
    tm = x_ref.shape[0]
    w3 = 3 * B_WIDTH
    c0 = 2 * A_WIDTH

    @pl.when(pl.program_id(0) == 0)
    def _():
        def store_rows(r):
            def consume(v):
                w_ref[r * PROJ_WR:(r + 1) * PROJ_WR, :] = _permute_w_in(v)
            return consume
        _stream_in([(w_hbm.at[l, pl.ds(r * PROJ_WR, PROJ_WR), :], w_stage, 0, store_rows(r))
                    for r in range(D_MODEL // PROJ_WR)], sem)

    @pl.when(pl.program_id(0) % tiles_per_seq == 0)
    def _():
        xbuf[0:TAIL, :] = jnp.zeros((TAIL, w3), F32)

    xb = x_ref[...].astype(BF16)
    xbuf[TAIL:TAIL + tm, :] = _dot(xb, w_ref[:, c0:c0 + w3])
    y = cw_ref[B_CONV - 1:B_CONV, :] * xbuf[TAIL:TAIL + tm, :]
    for i in range(B_CONV - 1):
        back = B_CONV - 1 - i
        y = y + cw_ref[i:i + 1, :] * xbuf[TAIL - back:TAIL - back + tm, :]
    xbuf[0:TAIL, :] = xbuf[tm:tm + TAIL, :]
    y = _silu(y)
    h_ref[:, 2 * B_WIDTH:w3] = y[:, 2 * B_WIDTH:w3]

    h_ref[:, w3:] = _dot(xb, w_ref[:, c0 + w3:])
    ha = _dot(xb, w_ref[:, 0:c0])
    u = _gelu_tanh(ha[:, 0:A_WIDTH])
    v = _layer_norm(_gelu_tanh(ha[:, A_WIDTH:2 * A_WIDTH]), lng_ref[...], lnb_ref[...])
    tril = _iota((A_CHUNK, A_CHUNK), 1) <= _iota((A_CHUNK, A_CHUNK), 0)
    head_of_lane = _iota((A_CHUNK, A_WIDTH), 1) // HEAD_DIM
    wts = [jnp.where(tril, sw_ref[h], 0.0).astype(BF16) for h in range(A_HEADS)]
    bias = bias_ref[...]
    for c in range(tm // A_CHUNK):
        rows = slice(c * A_CHUNK, (c + 1) * A_CHUNK)
        vc = v[rows]
        z = bias
        for h in range(A_HEADS):
            z = z + _dot(wts[h], jnp.where(head_of_lane == h, vc, 0.0).astype(BF16))
        a_ref[rows, :] = (u[rows] * z).astype(a_ref.dtype)

    ones2 = _group_ones(LANES, B_DK)
    for p in range(N_PAIRS):
        sl = slice(p * LANES, (p + 1) * LANES)
        yq = y[:, sl]
        h_ref[:, sl] = yq * (lax.rsqrt(_group_sumsq(yq, ones2) + 1e-6) * (B_DK ** -0.5))
        ksl = slice(B_WIDTH + p * LANES, B_WIDTH + (p + 1) * LANES)
        yk = y[:, ksl]
        h_ref[:, ksl] = yk * lax.rsqrt(_group_sumsq(yk, ones2) + 1e-6)


def _proj_sgu(x, w, lng, lnb, sw, bias, cw, seq, l):
    n = x.shape[0]
    tm = min(PROJ_TM, seq)
    return pl.pallas_call(
        functools.partial(_proj_sgu_kernel, seq // tm, l),
        grid=(n // tm,),
        in_specs=[
            pl.BlockSpec((tm, D_MODEL), lambda i: (i, 0)),
            HBM,
            _layer_spec((1, A_WIDTH), l),
            _layer_spec((1, A_WIDTH), l),
            _layer_spec((A_HEADS, A_CHUNK, A_CHUNK), l),
            _layer_spec((A_CHUNK, A_WIDTH), l),
            _layer_spec((B_CONV, 3 * B_WIDTH), l),
        ],
        out_specs=[pl.BlockSpec((tm, D_H), lambda i: (i, 0)), pl.BlockSpec((tm, A_WIDTH), lambda i: (i, 0))],
        out_shape=[jax.ShapeDtypeStruct((n, D_H), F32), jax.ShapeDtypeStruct((n, A_WIDTH), BF16)],
        scratch_shapes=[
            pltpu.VMEM((tm + TAIL, 3 * B_WIDTH), F32),
            pltpu.VMEM((D_MODEL, D_IN_PAD), BF16),
            pltpu.VMEM((2, PROJ_WR, D_IN), F32),
            pltpu.SemaphoreType.DMA((1, 2)),
        ],
        compiler_params=pltpu.CompilerParams(dimension_semantics=("arbitrary",), vmem_limit_bytes=VMEM_LIMIT),
        name="proj_sgu",
    )(x, w, lng, lnb, sw, bias, cw)


GLA_TB = 256


def _gla_kernel(q_ref, k_ref, v_ref, r_ref, g_ref, gup_ref, gb_ref, tri_ref, ng_ref, o_ref, st_ref, ob_ref):
    nb, tb = q_ref.shape[0], q_ref.shape[1]
    nc = tb // CHUNK

    @pl.when(pl.program_id(0) == 0)
    def _():
        st_ref[...] = jnp.zeros_like(st_ref)

    khead = _iota((CHUNK, C_KEY), 1) // C_DK
    vhead = _iota((CHUNK, C_WIDTH), 1) // C_DV
    causal = (_iota((CHUNK, C_WIDTH), 1) % CHUNK) <= _iota((CHUNK, C_WIDTH), 0)
    st_mask = (_iota((C_WIDTH, C_KEY), 0) // C_DV) == (_iota((C_WIDTH, C_KEY), 1) // C_DK)

    units = []
    for b in range(nb):
        logit = _dot(g_ref[b], gup_ref[...], precision=HIGHEST) + gb_ref[...]
        log_a = -_softplus(-logit) * (1.0 / C_TAU)
        g3 = _dot(tri_ref[...], _split3(log_a, 1))
        bc_all = g3[:, 0:C_KEY] + g3[:, C_KEY:2 * C_KEY] + g3[:, 2 * C_KEY:3 * C_KEY]
        for c in range(nc):
            rows = slice(c * CHUNK, (c + 1) * CHUNK)
            bc = bc_all[rows]
            btot = bc[CHUNK - 1:CHUNK, :]
            k = k_ref[b, rows, :]
            kn = k * jnp.exp(-bc)
            vb = v_ref[b, rows, :].astype(BF16)
            units.append(dict(
                b=b, rows=rows, vb=vb, gt=jnp.exp(btot),
                qd=(q_ref[b, rows, :] * ((C_DK ** -0.5) * jnp.exp(bc))).astype(BF16),
                kd=(k * jnp.exp(btot - bc)).astype(BF16),
                kstack=jnp.concatenate([jnp.where(khead == h, kn, 0.0) for h in range(C_HEADS)], axis=0).astype(BF16),
                vbd=jnp.concatenate([jnp.where(vhead == h, vb, jnp.zeros_like(vb)) for h in range(C_HEADS)], axis=0)))
    for u in units:
        u["p"] = jnp.where(causal, _dot_nt(u["qd"], u["kstack"]), 0.0).astype(BF16)
    for u in units:
        u["upd"] = jnp.where(st_mask, _dot_tn(u["vb"], u["kd"]), 0.0)
    for u in units:
        u["oi"] = _dot(u["p"], u["vbd"])

    states = [st_ref[b] for b in range(nb)]
    for u in units:
        b = u["b"]
        ob_ref[b, u["rows"], :] = u["oi"] + _dot_nt(u["qd"], states[b].astype(BF16))
        states[b] = u["gt"] * states[b] + u["upd"]
    for b in range(nb):
        st_ref[b] = states[b]

    ones = _group_ones(C_WIDTH, C_DV)
    for b in range(nb):
        o = ob_ref[b]
        ss = _group_sumsq(o, ones)
        o_ref[b] = (o * lax.rsqrt(ss * (1.0 / C_DV) + EPS) * ng_ref[...] * _silu(r_ref[b])).astype(o_ref.dtype)


def _gla(h3, gup, gb, ng, l):
    batch, seq, _ = h3.shape
    tb = min(GLA_TB, seq)
    hspec = lambda width, col: pl.BlockSpec((batch, tb, width), lambda t: (0, t, col))
    return pl.pallas_call(
        _gla_kernel,
        grid=(seq // tb,),
        in_specs=[
            hspec(C_KEY, COL_CQ), hspec(C_KEY, COL_CK), hspec(C_WIDTH, COL_CV), hspec(C_WIDTH, COL_CR),
            hspec(GATE_W, COL_G),
            _layer_spec((GATE_W, C_KEY), l),
            _layer_spec((1, C_KEY), l),
            pl.BlockSpec((tb, tb), lambda t: (0, 0)),
            _layer_spec((1, C_WIDTH), l),
        ],
        out_specs=pl.BlockSpec((batch, tb, C_WIDTH), lambda t: (0, t, 0)),
        out_shape=jax.ShapeDtypeStruct((batch, seq, C_WIDTH), BF16),
        scratch_shapes=[pltpu.VMEM((batch, C_WIDTH, C_KEY), F32), pltpu.VMEM((batch, tb, C_WIDTH), F32)],
        compiler_params=pltpu.CompilerParams(dimension_semantics=("arbitrary",), vmem_limit_bytes=VMEM_LIMIT),
        name="gla",
    )(h3, h3, h3, h3, h3, gup, gb, _chunk_triangles(tb)[0], ng)


GDN_TB = 256
GROUP_CHUNKS = 4


def _gdn_kernel(q_ref, k_ref, v_ref, z_ref, g_ref, arow_ref, dtrow_ref, ebg_ref, tri_ref, ng_ref, o_ref,
                bg_s, rows_s, qa_s, bb_s, gt_s, o_s, s_ref):
    nb, tb = q_ref.shape[0], q_ref.shape[1]
    nc = tb // CHUNK

    @pl.when(pl.program_id(0) == 0)
    def _():
        s_ref[...] = jnp.zeros_like(s_ref)

    ones2 = _group_ones(LANES, B_DK)
    lane = _iota((tb, GATE_W), 1)
    low_half = _iota((1, LANES), 1) < CHUNK
    for b in range(nb):
        gates = g_ref[b]
        beta = jax.nn.sigmoid(gates)
        lg = jnp.where((lane >= B_HEADS) & (lane < 2 * B_HEADS),
                       -jnp.exp(arow_ref[...]) * _softplus(gates + dtrow_ref[...]), 0.0)
        g3 = _dot(tri_ref[0], _split3(lg, 1))
        gam = g3[:, 0:GATE_W] + g3[:, GATE_W:2 * GATE_W] + g3[:, 2 * GATE_W:3 * GATE_W]
        feat = jnp.where(lane < B_HEADS, beta, gam)
        bg_s[b] = _dot(_split2(feat), ebg_ref[...])

        lg_t3 = _split3(lg.T[0:2 * B_HEADS, :], 0)
        nh2 = 2 * B_HEADS
        t3 = _dot(lg_t3, tri_ref[1])
        gam_t = t3[0:nh2] + t3[nh2:2 * nh2] + t3[2 * nh2:3 * nh2]
        t3 = _dot(lg_t3, tri_ref[2])
        gam_t_sw = t3[0:nh2] + t3[nh2:2 * nh2] + t3[2 * nh2:3 * nh2]
        for c in range(nc):
            tile = slice((c // 2) * LANES, (c // 2 + 1) * LANES)
            first, second = (gam_t, gam_t_sw) if c % 2 == 0 else (gam_t_sw, gam_t)
            for p in range(N_PAIRS):
                h0 = B_HEADS + 2 * p
                rows_s[b, c, p:p + 1, :] = jnp.where(low_half, first[h0:h0 + 1, tile], second[h0 + 1:h0 + 2, tile])

    ii = _iota((CHUNK, LANES), 0)
    jj = _iota((CHUNK, LANES), 1) % CHUNK
    incl = jj <= ii
    strict = jj < ii
    eye = jnp.where(jj == ii, 1.0, 0.0)
    lo_lane = _iota((CHUNK, LANES), 1) < CHUNK
    bd_mask = (_iota((LANES, LANES), 0) // CHUNK) == (_iota((LANES, LANES), 1) // CHUNK)

    def blockdiag(m):
        zero = jnp.zeros_like(m)
        return jnp.concatenate([jnp.where(lo_lane, m, zero), jnp.where(lo_lane, zero, m)], axis=0)

    groups_per_batch = nc // GROUP_CHUNKS

    def solve_group(g, carry):
        b = g // groups_per_batch
        c0 = (g % groups_per_batch) * GROUP_CHUNKS
        units = []
        for dc in range(GROUP_CHUNKS):
            c = c0 + dc
            rws = pl.ds(pl.multiple_of(c * CHUNK, CHUNK), CHUNK)
            grows = rows_s[b, c]
            for p in range(N_PAIRS):
                sl = slice(p * LANES, (p + 1) * LANES)
                q = q_ref[b, rws, sl]
                k = k_ref[b, rws, sl]
                v = v_ref[b, rws, sl]
                bet = bg_s[b, rws, sl]
                gi = bg_s[b, rws, B_WIDTH + p * LANES:B_WIDTH + (p + 1) * LANES]
                dec = jnp.exp(jnp.minimum(gi - grows[p:p + 1, :], 0.0))
                eg = jnp.exp(gi)
                glast = gi[CHUNK - 1:CHUNK, :]
                k16 = k.astype(BF16)
                qk = _dot_nt(jnp.concatenate([q.astype(BF16), k16], axis=0), blockdiag(k16))
                nk = jnp.where(strict, qk[CHUNK:2 * CHUNK] * (-bet) * dec, 0.0)
                gt_s[b, c, p:p + 1, :] = jnp.exp(glast)
                units.append(dict(
                    c=c, p=p, rws=rws, sl=sl,
                    pm=jnp.where(incl, qk[0:CHUNK] * dec, 0.0).astype(BF16),
                    tm=eye + nk, nk=nk.astype(BF16),
                    rhs=jnp.concatenate([blockdiag((v * bet).astype(BF16)),
                                         blockdiag((k * (bet * eg)).astype(BF16))], axis=1),
                    qd=q * eg, kd=(k * jnp.exp(glast - gi)).astype(BF16)))
        for u in units:
            u["nk"] = _dot(u["nk"], blockdiag(u["nk"])).astype(BF16)
        for _ in range(4):
            for u in units:
                both = _dot(u["nk"], jnp.concatenate([blockdiag(u["nk"]), blockdiag(u["tm"].astype(BF16))], axis=1))
                u["nk"] = both[:, 0:LANES].astype(BF16)
                u["tm"] = u["tm"] + both[:, LANES:2 * LANES]
        for u in units:
            u["tm"] = u["tm"] + _dot(u["nk"], blockdiag(u["tm"].astype(BF16)))
        for u in units:
            u["sol"] = _dot(u["tm"].astype(BF16), u["rhs"]).astype(BF16)
        for u in units:
            ab = _dot_tn(u["kd"], u["sol"])
            bb_s[b, u["c"], u["p"]] = jnp.where(bd_mask, ab[:, 0:LANES], 0.0)
            qa_s[b, u["c"], u["p"], CHUNK:CHUNK + LANES, :] = jnp.where(bd_mask, -ab[:, LANES:2 * LANES], 0.0).astype(BF16)
        for u in units:
            sol = u["sol"]
            po = _dot(u["pm"], jnp.concatenate([blockdiag(sol[:, 0:LANES]), blockdiag(sol[:, LANES:2 * LANES])], axis=1))
            o_s[b, u["rws"], u["sl"]] = po[:, 0:LANES]
            qa_s[b, u["c"], u["p"], 0:CHUNK, :] = (u["qd"] - po[:, LANES:2 * LANES]).astype(BF16)
        return carry

    lax.fori_loop(0, nb * groups_per_batch, solve_group, 0)

    def scan_chunk(c, carry):
        rws = pl.ds(pl.multiple_of(c * CHUNK, CHUNK), CHUNK)
        chains = [(b, p) for b in range(nb) for p in range(N_PAIRS)]
        states = [s_ref[b, p] for b, p in chains]
        prods = [_dot(qa_s[b, c, p], s.astype(BF16)) for (b, p), s in zip(chains, states)]
        for (b, p), s, r in zip(chains, states, prods):
            sl = slice(p * LANES, (p + 1) * LANES)
            s_ref[b, p] = gt_s[b, c, p:p + 1, :] * s + r[CHUNK:CHUNK + LANES] + bb_s[b, c, p]
            o_s[b, rws, sl] = o_s[b, rws, sl] + r[0:CHUNK]
        return carry

    lax.fori_loop(0, nc, scan_chunk, 0)

    for b in range(nb):
        for p in range(N_PAIRS):
            sl = slice(p * LANES, (p + 1) * LANES)
            o = o_s[b, :, sl]
            ss = _group_sumsq(o, ones2)
            o_ref[b, :, sl] = (o * lax.rsqrt(ss * (1.0 / 64.0) + EPS) * ng_ref[...]
                               * _silu(z_ref[b, :, sl])).astype(o_ref.dtype)


def _chunk_triangles(tb):
    r = np.arange(tb)[:, None]
    c = np.arange(tb)[None, :]
    same = (r // CHUNK) == (c // CHUNK)
    c_sw = np.where((c % LANES) < CHUNK, c + CHUNK, c - CHUNK)
    tril = same & (c <= r)
    triu = same & (r <= c)
    triu_sw = ((r // CHUNK) == (c_sw // CHUNK)) & (r <= c_sw)
    return jnp.asarray(np.stack([tril, triu, triu_sw]).astype(np.float32), BF16)


def _gdn(h3, arow, dtrow, ebg, ng, l):
    batch, seq, _ = h3.shape
    tb = min(GDN_TB, seq)
    nc = tb // CHUNK
    hspec = lambda width, col: pl.BlockSpec((batch, tb, width), lambda t: (0, t, col))
    return pl.pallas_call(
        _gdn_kernel,
        grid=(seq // tb,),
        in_specs=[
            hspec(B_WIDTH, COL_BQ), hspec(B_WIDTH, COL_BK), hspec(B_WIDTH, COL_BV), hspec(B_WIDTH, COL_BZ),
            hspec(GATE_W, COL_G),
            _layer_spec((1, GATE_W), l),
            _layer_spec((1, GATE_W), l),
            pl.BlockSpec((2 * GATE_W, 2 * B_WIDTH), lambda t: (0, 0)),
            pl.BlockSpec((3, tb, tb), lambda t: (0, 0, 0)),
            _layer_spec((1, LANES), l),
        ],
        out_specs=pl.BlockSpec((batch, tb, B_WIDTH), lambda t: (0, t, 0)),
        out_shape=jax.ShapeDtypeStruct((batch, seq, B_WIDTH), BF16),
        scratch_shapes=[
            pltpu.VMEM((batch, tb, 2 * B_WIDTH), F32),
            pltpu.VMEM((batch, nc, 8, LANES), F32),
            pltpu.VMEM((batch, nc, N_PAIRS, CHUNK + LANES, LANES), BF16),
            pltpu.VMEM((batch, nc, N_PAIRS, LANES, LANES), F32),
            pltpu.VMEM((batch, nc, 8, LANES), F32),
            pltpu.VMEM((batch, tb, B_WIDTH), F32),
            pltpu.VMEM((batch, N_PAIRS, LANES, LANES), F32),
        ],
        compiler_params=pltpu.CompilerParams(dimension_semantics=("arbitrary",), vmem_limit_bytes=VMEM_LIMIT),
        name="gdn",
    )(h3, h3, h3, h3, h3, arow, dtrow, ebg, _chunk_triangles(tb), ng)


def _out_ffn_kernel(l, x_ref, a_ref, b_ref, c_ref, wo_hbm, g2_ref, b2_ref, w1_hbm, w3_hbm, w2_hbm, g3_ref, b3_ref,
                    o_ref, act_ref, w1_ref, w3_ref, w2_ref, col_stage, row_stage, wo_ref, sem):
    @pl.when(pl.program_id(0) == 0)
    def _():
        plan = [(wo_hbm.at[l, pl.ds(j * FFN_TF, FFN_TF), :], row_stage, 1,
                 _store_cast(wo_ref, slice(j * FFN_TF, (j + 1) * FFN_TF), slice(None)))
                for j in range(D_MODEL // FFN_TF)]
        plan += _ffn_weight_plan(l, w1_hbm, w3_hbm, w2_hbm, w1_ref, w3_ref, w2_ref, col_stage, row_stage)
        _stream_in(plan, sem)

    slabs = _row_slabs(x_ref.shape[0])
    mixes = []
    for rows in slabs:
        mix = _dot(a_ref[rows, :], wo_ref[0:A_WIDTH, :])
        mix = mix + _dot(b_ref[rows, :], wo_ref[A_WIDTH:A_WIDTH + B_WIDTH, :])
        mixes.append(mix + _dot(c_ref[rows, :], wo_ref[A_WIDTH + B_WIDTH:, :]))
    for rows, mix in zip(slabs, mixes):
        x2 = _layer_norm(ALPHA * x_ref[rows, :] + mix, g2_ref[...], b2_ref[...])
        o_ref[rows, :] = _ffn_ln_rows(x2, rows, w1_ref, w3_ref, w2_ref, g3_ref, b3_ref, act_ref)


def _out_ffn(x, a, b, c, wo, g2, b2, w1, w3, w2, g3, b3, l):
    n = x.shape[0]
    tm = min(FFN_TM, n)
    rows = lambda width: pl.BlockSpec((tm, width), lambda i: (i, 0))
    return pl.pallas_call(
        functools.partial(_out_ffn_kernel, l),
        grid=(n // tm,),
        in_specs=[
            rows(D_MODEL), rows(A_WIDTH), rows(B_WIDTH), rows(C_WIDTH),
            HBM,
            _layer_spec((1, D_MODEL), l),
            _layer_spec((1, D_MODEL), l),
            HBM, HBM, HBM,
            _layer_spec((1, D_MODEL), l),
            _layer_spec((1, D_MODEL), l),
        ],
        out_specs=rows(D_MODEL),
        out_shape=jax.ShapeDtypeStruct((n, D_MODEL), F32),
        scratch_shapes=_ffn_scratch(tm) + [pltpu.VMEM((D_MODEL, D_MODEL), BF16),
                                           pltpu.SemaphoreType.DMA((2, 2))],
        compiler_params=pltpu.CompilerParams(dimension_semantics=("arbitrary",), vmem_limit_bytes=VMEM_LIMIT),
        name="out_ffn",
    )(x, a, b, c, wo, g2, b2, w1, w3, w2, g3, b3)


def _bcast_selector():
    e = np.zeros((2 * GATE_W, 2 * B_WIDTH), np.float32)
    for j in range(B_WIDTH):
        hd = j // 64
        for base in (0, GATE_W):
            e[base + hd, j] = 1.0
            e[base + B_HEADS + hd, B_WIDTH + j] = 1.0
    return jnp.asarray(e, BF16)


def _gate_row(vec, offset):
    l, n = vec.shape
    return jnp.zeros((l, 1, GATE_W), F32).at[:, 0, offset:offset + n].set(vec)


def kernel(x, ffn1_w1, ffn1_w3, ffn1_w2, ln1_g, ln1_b, w_in, sgu_ln_g, sgu_ln_b, sgu_w, sgu_b, gdn_conv_w, gdn_a_log, gdn_dt_bias, gdn_norm_g, gla_gate_up, gla_gate_b, gla_norm_g, w_out, ln2_g, ln2_b, ffn2_w1, ffn2_w3, ffn2_w2, ln3_g, ln3_b):
    batch, seq, d = x.shape
    depth = w_in.shape[0]
    row = lambda p: p[:, None, :]
    f1 = (ffn1_w1, ffn1_w3, ffn1_w2)
    f2 = (ffn2_w1, ffn2_w3, ffn2_w2)
    sgu_bias = jnp.repeat(jnp.swapaxes(sgu_b, 1, 2), HEAD_DIM, axis=2)
    arow = _gate_row(gdn_a_log, B_HEADS)
    dtrow = _gate_row(gdn_dt_bias, B_HEADS)
    ebg = _bcast_selector()
    gdn_ng = jnp.tile(gdn_norm_g, (1, 2))[:, None, :]
    gla_ng = jnp.tile(gla_norm_g, (1, C_HEADS))[:, None, :]
    gup = jnp.zeros((depth, GATE_W, C_KEY), F32).at[:, 2 * B_HEADS:2 * B_HEADS + C_RANK, :].set(gla_gate_up)

    xf = x.reshape(batch * seq, d)
    for l in range(depth):
        xf = _ffn_ln(xf, *f1, row(ln1_g), row(ln1_b), l)
        h, out_a = _proj_sgu(xf, w_in, row(sgu_ln_g), row(sgu_ln_b), sgu_w, sgu_bias, gdn_conv_w, seq, l)
        h3 = h.reshape(batch, seq, D_H)
        out_b = _gdn(h3, arow, dtrow, ebg, gdn_ng, l).reshape(batch * seq, B_WIDTH)
        out_c = _gla(h3, gup, row(gla_gate_b), gla_ng, l).reshape(batch * seq, C_WIDTH)
        xf = _out_ffn(xf, out_a, out_b, out_c, w_out, row(ln2_g), row(ln2_b), *f2, row(ln3_g), row(ln3_b), l)
    return xf.reshape(batch, seq, d)
```

```python
import functools

import numpy as np
import jax
import jax.numpy as jnp
from jax import lax
from jax.experimental import pallas as pl
from jax.experimental.pallas import tpu as pltpu

F32 = jnp.float32
BF16 = jnp.bfloat16
HIGHEST = lax.Precision.HIGHEST

D_MODEL = 1024
DEPTH = 4
HEAD_DIM = 64
A_HEADS = 4
A_WIDTH = A_HEADS * HEAD_DIM
A_CHUNK = 128
B_HEADS = 8
B_DK = 64
B_WIDTH = B_HEADS * 64
B_CONV = 4
C_HEADS = 4
C_DK = 32
C_DV = 64
C_KEY = C_HEADS * C_DK
C_WIDTH = C_HEADS * C_DV
C_RANK = 16
C_TAU = 16.0
D_FF = 2816
ALPHA = (2.0 * DEPTH) ** 0.25
EPS = 1e-5

LANES = 128
CHUNK = 64
GATE_W = LANES
N_PAIRS = B_HEADS // 2
TAIL = 8
D_IN_PAD = 2 * A_WIDTH + 4 * B_WIDTH + 2 * C_WIDTH + 2 * C_KEY + GATE_W
VMEM_LIMIT = 56 * 1024 * 1024

D_H = D_IN_PAD - 2 * A_WIDTH

COL_BQ, COL_BK, COL_BV, COL_BZ = 0, 1, 2, 3
COL_CV, COL_CR = 8, 9
COL_CQ, COL_CK, COL_G = 20, 21, 22


def _dot(a, b, **kw):
    return jnp.dot(a, b, preferred_element_type=F32, **kw)


def _dot_nt(a, b, **kw):
    return lax.dot_general(a, b, (((1,), (1,)), ((), ())), preferred_element_type=F32, **kw)


def _dot_tn(a, b, **kw):
    return lax.dot_general(a, b, (((0,), (0,)), ((), ())), preferred_element_type=F32, **kw)


def _split2(x):
    hi = x.astype(BF16)
    lo = (x - hi.astype(F32)).astype(BF16)
    return jnp.concatenate([hi, lo], axis=-1)


def _split3(x, axis):
    hi = x.astype(BF16)
    r1 = x - hi.astype(F32)
    mid = r1.astype(BF16)
    lo = (r1 - mid.astype(F32)).astype(BF16)
    return jnp.concatenate([hi, mid, lo], axis=axis)


def _iota(shape, dim):
    return lax.broadcasted_iota(jnp.int32, shape, dim)


def _group_ones(width, group):
    r = _iota((width, width), 0)
    c = _iota((width, width), 1)
    return jnp.where((r // group) == (c // group), 1.0, 0.0).astype(BF16)


def _group_sumsq(x, ones):
    return _dot((x * x).astype(BF16), ones)


def _layer_norm(y, g, b):
    mu = jnp.mean(y, axis=-1, keepdims=True)
    yc = y - mu
    var = jnp.mean(yc * yc, axis=-1, keepdims=True)
    return yc * lax.rsqrt(var + EPS) * g + b


def _silu(x):
    return x * jax.nn.sigmoid(x)


def _gelu_tanh(x):
    return 0.5 * x * (1.0 + jnp.tanh(0.7978845608028654 * (x + 0.044715 * (x * x * x))))


def _softplus(x):
    return jnp.maximum(x, 0.0) + jnp.log(1.0 + jnp.exp(-jnp.abs(x)))


FFN_TM = 1024
FFN_TF = 256
FFN_SLABS = 4


N_FF_CHUNKS = D_FF // FFN_TF


def _row_slabs(tm):
    step = tm // FFN_SLABS
    return [slice(r * step, (r + 1) * step) for r in range(FFN_SLABS)]


def _ffn_up(x_ref, slabs, chunks, w1b, w3b, a_ref):
    for rows in slabs:
        xb = x_ref[rows, :].astype(BF16)
        for j in chunks:
            sl = slice(j * FFN_TF, (j + 1) * FFN_TF)
            h1 = _dot(xb, w1b[:, sl])
            h3 = _dot(xb, w3b[:, sl])
            a_ref[rows, sl] = (_silu(h1) * h3).astype(BF16)


def _ffn_down_ln(x_ref, slabs, w2b, g_ref, b_ref, a_ref, o_ref):
    for rows in slabs:
        y = ALPHA * x_ref[rows, :] + 0.5 * _dot(a_ref[rows, :], w2b[...])
        o_ref[rows, :] = _layer_norm(y, g_ref[...], b_ref[...])


def _ffn_from_hbm(l, first_step, x_ref, w1_hbm, w3_hbm, w2_hbm, g_ref, b_ref, o_ref,
                  a_ref, w1b, w3b, w2b, col_stage, row_stage, sem):
    slabs = _row_slabs(x_ref.shape[0])

    @pl.when(first_step)
    def _():
        after = lambda j: _ffn_up(x_ref, slabs, [j], w1b, w3b, a_ref)
        _stream_in(_ffn_weight_plan(l, w1_hbm, w3_hbm, w2_hbm, w1b, w3b, w2b, col_stage, row_stage, after), sem)

    @pl.when(jnp.logical_not(first_step))
    def _():
        _ffn_up(x_ref, slabs, range(N_FF_CHUNKS), w1b, w3b, a_ref)

    _ffn_down_ln(x_ref, slabs, w2b, g_ref, b_ref, a_ref, o_ref)


def _stream_in(plan, sem):
    used = {}
    steps = []
    for src, stage, sid, consume in plan:
        slot = used.get(sid, 0) % 2
        used[sid] = used.get(sid, 0) + 1
        steps.append((pltpu.make_async_copy(src, stage.at[slot], sem.at[sid, slot]), stage, slot, consume))
    steps[0][0].start()
    for k, (copy, stage, slot, consume) in enumerate(steps):
        if k + 1 < len(steps):
            steps[k + 1][0].start()
        copy.wait()
        consume(stage[slot])


def _store_cast(dst, rows, cols, then=None):
    def consume(v):
        dst[rows, cols] = v.astype(BF16)
        if then is not None:
            then()
    return consume


def _ffn_weight_plan(l, w1_hbm, w3_hbm, w2_hbm, w1b, w3b, w2b, col_stage, row_stage, after_chunk):
    plan = []
    full = slice(None)
    for j in range(N_FF_CHUNKS):
        win = pl.ds(j * FFN_TF, FFN_TF)
        sl = slice(j * FFN_TF, (j + 1) * FFN_TF)
        plan.append((w1_hbm.at[l, :, win], col_stage, 0, _store_cast(w1b, full, sl)))
        plan.append((w2_hbm.at[l, win, :], row_stage, 1, _store_cast(w2b, sl, full)))
        plan.append((w3_hbm.at[l, :, win], col_stage, 0,
                     _store_cast(w3b, full, sl, functools.partial(after_chunk, j))))
    return plan


def _ffn_scratch(tm):
    return [
        pltpu.VMEM((tm, D_FF), BF16),
        pltpu.VMEM((D_MODEL, D_FF), BF16),
        pltpu.VMEM((D_MODEL, D_FF), BF16),
        pltpu.VMEM((D_FF, D_MODEL), BF16),
        pltpu.VMEM((2, D_MODEL, FFN_TF), F32),
        pltpu.VMEM((2, FFN_TF, D_MODEL), F32),
    ]


def _ffn_ln_kernel(l, x_ref, w1_hbm, w3_hbm, w2_hbm, g_ref, b_ref, o_ref, *scratch):
    _ffn_from_hbm(l, pl.program_id(0) == 0, x_ref, w1_hbm, w3_hbm, w2_hbm, g_ref, b_ref, o_ref, *scratch)


def _layer_spec(shape, l, **kw):
    zeros = (0,) * len(shape)
    return pl.BlockSpec((None,) + tuple(shape), lambda *_: (l,) + zeros, **kw)


HBM = pl.BlockSpec(memory_space=pl.ANY)


def _ffn_ln(x, w1, w3, w2, g, b, l):
    n = x.shape[0]
    tm = min(FFN_TM, n)
    return pl.pallas_call(
        functools.partial(_ffn_ln_kernel, l),
        grid=(n // tm,),
        in_specs=[
            pl.BlockSpec((tm, D_MODEL), lambda i: (i, 0)),
            HBM, HBM, HBM,
            _layer_spec((1, D_MODEL), l),
            _layer_spec((1, D_MODEL), l),
        ],
        out_specs=pl.BlockSpec((tm, D_MODEL), lambda i: (i, 0)),
        out_shape=jax.ShapeDtypeStruct((n, D_MODEL), F32),
        scratch_shapes=_ffn_scratch(tm) + [pltpu.SemaphoreType.DMA((2, 2))],
        compiler_params=pltpu.CompilerParams(dimension_semantics=("arbitrary",), vmem_limit_bytes=VMEM_LIMIT),
        name="ffn_ln",
    )(x, w1, w3, w2, g, b)


PROJ_TM = 512
PROJ_WR = 128
D_IN = 2 * A_WIDTH + 4 * B_WIDTH + 2 * B_HEADS + 2 * C_KEY + 2 * C_WIDTH + C_RANK


def _permute_w_in(v):
    o = np.cumsum([0, A_WIDTH, A_WIDTH, 3 * B_WIDTH, B_WIDTH, B_HEADS, B_HEADS, C_KEY, C_KEY, C_WIDTH, C_WIDTH, C_RANK])
    front = v[:, 0:o[4]]
    gates_b = v[:, o[4]:o[6]]
    c_qk = v[:, o[6]:o[8]]
    c_vr = v[:, o[8]:o[10]]
    c_g = v[:, o[10]:o[11]]
    pad = jnp.zeros((v.shape[0], GATE_W - 2 * B_HEADS - C_RANK), F32)
    return jnp.concatenate([front, c_vr, c_qk, gates_b, c_g, pad], axis=-1).astype(BF16)


def _proj_sgu_kernel(tiles_per_seq, l, x_ref, w_hbm, lng_ref, lnb_ref, sw_ref, bias_ref, cw_ref, h_ref, a_ref,
                     xbuf, w_ref, w_stage, sem):
    tm = x_ref.shape[0]
    w3 = 3 * B_WIDTH
    c0 = 2 * A_WIDTH

    @pl.when(pl.program_id(0) == 0)
    def _():
        def store_rows(r):
            def consume(v):
                w_ref[r * PROJ_WR:(r + 1) * PROJ_WR, :] = _permute_w_in(v.astype(F32))
            return consume
        _stream_in([(w_hbm.at[l, pl.ds(r * PROJ_WR, PROJ_WR), :], w_stage, 0, store_rows(r))
                    for r in range(D_MODEL // PROJ_WR)], sem)

    @pl.when(pl.program_id(0) % tiles_per_seq == 0)
    def _():
        xbuf[0:TAIL, :] = jnp.zeros((TAIL, w3), F32)

    xb = x_ref[...].astype(BF16)
    xbuf[TAIL:TAIL + tm, :] = _dot(xb, w_ref[:, c0:c0 + w3])
    y = cw_ref[B_CONV - 1:B_CONV, :] * xbuf[TAIL:TAIL + tm, :]
    for i in range(B_CONV - 1):
        back = B_CONV - 1 - i
        y = y + cw_ref[i:i + 1, :] * xbuf[TAIL - back:TAIL - back + tm, :]
    xbuf[0:TAIL, :] = xbuf[tm:tm + TAIL, :]
    y = _silu(y)
    h_ref[:, 2 * B_WIDTH:w3] = y[:, 2 * B_WIDTH:w3]

    h_ref[:, w3:] = _dot(xb, w_ref[:, c0 + w3:])
    ha = _dot(xb, w_ref[:, 0:c0])
    u = _gelu_tanh(ha[:, 0:A_WIDTH])
    v = _layer_norm(_gelu_tanh(ha[:, A_WIDTH:2 * A_WIDTH]), lng_ref[...], lnb_ref[...])
    tril = _iota((A_CHUNK, A_CHUNK), 1) <= _iota((A_CHUNK, A_CHUNK), 0)
    head_of_lane = _iota((A_CHUNK, A_WIDTH), 1) // HEAD_DIM
    wts = [jnp.where(tril, sw_ref[h], 0.0).astype(BF16) for h in range(A_HEADS)]
    bias = bias_ref[...]
    for c in range(tm // A_CHUNK):
        rows = slice(c * A_CHUNK, (c + 1) * A_CHUNK)
        vc = v[rows]
        z = bias
        for h in range(A_HEADS):
            z = z + _dot(wts[h], jnp.where(head_of_lane == h, vc, 0.0).astype(BF16))
        a_ref[rows, :] = (u[rows] * z).astype(a_ref.dtype)

    ones2 = _group_ones(LANES, B_DK)
    for p in range(N_PAIRS):
        sl = slice(p * LANES, (p + 1) * LANES)
        yq = y[:, sl]
        h_ref[:, sl] = yq * (lax.rsqrt(_group_sumsq(yq, ones2) + 1e-6) * (B_DK ** -0.5))
        ksl = slice(B_WIDTH + p * LANES, B_WIDTH + (p + 1) * LANES)
        yk = y[:, ksl]
        h_ref[:, ksl] = yk * lax.rsqrt(_group_sumsq(yk, ones2) + 1e-6)


def _proj_sgu(x, w, lng, lnb, sw, bias, cw, seq, l):
    n = x.shape[0]
    tm = min(PROJ_TM, seq)
    return pl.pallas_call(
        functools.partial(_proj_sgu_kernel, seq // tm, l),
        grid=(n // tm,),
        in_specs=[
            pl.BlockSpec((tm, D_MODEL), lambda i: (i, 0)),
            HBM,
            _layer_spec((1, A_WIDTH), l),
            _layer_spec((1, A_WIDTH), l),
            _layer_spec((A_HEADS, A_CHUNK, A_CHUNK), l),
            _layer_spec((A_CHUNK, A_WIDTH), l),
            _layer_spec((B_CONV, 3 * B_WIDTH), l),
        ],
        out_specs=[pl.BlockSpec((tm, D_H), lambda i: (i, 0)), pl.BlockSpec((tm, A_WIDTH), lambda i: (i, 0))],
        out_shape=[jax.ShapeDtypeStruct((n, D_H), F32), jax.ShapeDtypeStruct((n, A_WIDTH), BF16)],
        scratch_shapes=[
            pltpu.VMEM((tm + TAIL, 3 * B_WIDTH), F32),
            pltpu.VMEM((D_MODEL, D_IN_PAD), BF16),
            pltpu.VMEM((2, PROJ_WR, D_IN), BF16),
            pltpu.SemaphoreType.DMA((1, 2)),
        ],
        compiler_params=pltpu.CompilerParams(dimension_semantics=("arbitrary",), vmem_limit_bytes=VMEM_LIMIT),
        name="proj_sgu",
    )(x, w, lng, lnb, sw, bias, cw)


GLA_TB = 256


def _gla_kernel(q_ref, k_ref, v_ref, r_ref, g_ref, gup_ref, gb_ref, tri_ref, ng_ref, o_ref, st_ref, ob_ref):
    nb, tb = q_ref.shape[0], q_ref.shape[1]
    nc = tb // CHUNK

    @pl.when(pl.program_id(0) == 0)
    def _():
        st_ref[...] = jnp.zeros_like(st_ref)

    khead = _iota((CHUNK, C_KEY), 1) // C_DK
    vhead = _iota((CHUNK, C_WIDTH), 1) // C_DV
    causal = (_iota((CHUNK, C_WIDTH), 1) % CHUNK) <= _iota((CHUNK, C_WIDTH), 0)
    st_mask = (_iota((C_WIDTH, C_KEY), 0) // C_DV) == (_iota((C_WIDTH, C_KEY), 1) // C_DK)

    units = []
    for b in range(nb):
        logit = _dot(g_ref[b], gup_ref[...], precision=HIGHEST) + gb_ref[...]
        log_a = -_softplus(-logit) * (1.0 / C_TAU)
        g3 = _dot(tri_ref[...], _split3(log_a, 1))
        bc_all = g3[:, 0:C_KEY] + g3[:, C_KEY:2 * C_KEY] + g3[:, 2 * C_KEY:3 * C_KEY]
        for c in range(nc):
            rows = slice(c * CHUNK, (c + 1) * CHUNK)
            bc = bc_all[rows]
            btot = bc[CHUNK - 1:CHUNK, :]
            k = k_ref[b, rows, :]
            kn = k * jnp.exp(-bc)
            vb = v_ref[b, rows, :].astype(BF16)
            units.append(dict(
                b=b, rows=rows, vb=vb, gt=jnp.exp(btot),
                qd=(q_ref[b, rows, :] * ((C_DK ** -0.5) * jnp.exp(bc))).astype(BF16),
                kd=(k * jnp.exp(btot - bc)).astype(BF16),
                kstack=jnp.concatenate([jnp.where(khead == h, kn, 0.0) for h in range(C_HEADS)], axis=0).astype(BF16),
                vbd=jnp.concatenate([jnp.where(vhead == h, vb, jnp.zeros_like(vb)) for h in range(C_HEADS)], axis=0)))
    for u in units:
        u["p"] = jnp.where(causal, _dot_nt(u["qd"], u["kstack"]), 0.0).astype(BF16)
    for u in units:
        u["upd"] = jnp.where(st_mask, _dot_tn(u["vb"], u["kd"]), 0.0)
    for u in units:
        u["oi"] = _dot(u["p"], u["vbd"])

    states = [st_ref[b] for b in range(nb)]
    for u in units:
        b = u["b"]
        ob_ref[b, u["rows"], :] = u["oi"] + _dot_nt(u["qd"], states[b].astype(BF16))
        states[b] = u["gt"] * states[b] + u["upd"]
    for b in range(nb):
        st_ref[b] = states[b]

    ones = _group_ones(C_WIDTH, C_DV)
    for b in range(nb):
        o = ob_ref[b]
        ss = _group_sumsq(o, ones)
        o_ref[b] = (o * lax.rsqrt(ss * (1.0 / C_DV) + EPS) * ng_ref[...] * _silu(r_ref[b])).astype(o_ref.dtype)


def _gla(h3, gup, gb, ng, l):
    batch, seq, _ = h3.shape
    tb = min(GLA_TB, seq)
    hspec = lambda width, col: pl.BlockSpec((batch, tb, width), lambda t: (0, t, col))
    return pl.pallas_call(
        _gla_kernel,
        grid=(seq // tb,),
        in_specs=[
            hspec(C_KEY, COL_CQ), hspec(C_KEY, COL_CK), hspec(C_WIDTH, COL_CV), hspec(C_WIDTH, COL_CR),
            hspec(GATE_W, COL_G),
            _layer_spec((GATE_W, C_KEY), l),
            _layer_spec((1, C_KEY), l),
            pl.BlockSpec((tb, tb), lambda t: (0, 0)),
            _layer_spec((1, C_WIDTH), l),
        ],
        out_specs=pl.BlockSpec((batch, tb, C_WIDTH), lambda t: (0, t, 0)),
        out_shape=jax.ShapeDtypeStruct((batch, seq, C_WIDTH), BF16),
        scratch_shapes=[pltpu.VMEM((batch, C_WIDTH, C_KEY), F32), pltpu.VMEM((batch, tb, C_WIDTH), F32)],
        compiler_params=pltpu.CompilerParams(dimension_semantics=("arbitrary",), vmem_limit_bytes=VMEM_LIMIT),
        name="gla",
    )(h3, h3, h3, h3, h3, gup, gb, _chunk_triangles(tb)[0], ng)


GDN_TB = 256
GROUP_CHUNKS = 4


def _gdn_kernel(q_ref, k_ref, v_ref, z_ref, g_ref, arow_ref, dtrow_ref, ebg_ref, tri_ref, ng_ref, o_ref,
                bg_s, rows_s, qa_s, bb_s, gt_s, o_s, s_ref):
    nb, tb = q_ref.shape[0], q_ref.shape[1]
    nc = tb // CHUNK

    @pl.when(pl.program_id(0) == 0)
    def _():
        s_ref[...] = jnp.zeros_like(s_ref)

    ones2 = _group_ones(LANES, B_DK)
    lane = _iota((tb, GATE_W), 1)
    low_half = _iota((1, LANES), 1) < CHUNK
    for b in range(nb):
        gates = g_ref[b]
        beta = jax.nn.sigmoid(gates)
        lg = jnp.where((lane >= B_HEADS) & (lane < 2 * B_HEADS),
                       -jnp.exp(arow_ref[...]) * _softplus(gates + dtrow_ref[...]), 0.0)
        g3 = _dot(tri_ref[0], _split3(lg, 1))
        gam = g3[:, 0:GATE_W] + g3[:, GATE_W:2 * GATE_W] + g3[:, 2 * GATE_W:3 * GATE_W]
        feat = jnp.where(lane < B_HEADS, beta, gam)
        bg_s[b] = _dot(_split2(feat), ebg_ref[...])

        lg_t3 = _split3(lg.T[0:2 * B_HEADS, :], 0)
        nh2 = 2 * B_HEADS
        t3 = _dot(lg_t3, tri_ref[1])
        gam_t = t3[0:nh2] + t3[nh2:2 * nh2] + t3[2 * nh2:3 * nh2]
        t3 = _dot(lg_t3, tri_ref[2])
        gam_t_sw = t3[0:nh2] + t3[nh2:2 * nh2] + t3[2 * nh2:3 * nh2]
        for c in range(nc):
            tile = slice((c // 2) * LANES, (c // 2 + 1) * LANES)
            first, second = (gam_t, gam_t_sw) if c % 2 == 0 else (gam_t_sw, gam_t)
            for p in range(N_PAIRS):
                h0 = B_HEADS + 2 * p
                rows_s[b, c, p:p + 1, :] = jnp.where(low_half, first[h0:h0 + 1, tile], second[h0 + 1:h0 + 2, tile])

    ii = _iota((CHUNK, LANES), 0)
    jj = _iota((CHUNK, LANES), 1) % CHUNK
    incl = jj <= ii
    strict = jj < ii
    eye = jnp.where(jj == ii, 1.0, 0.0)
    lo_lane = _iota((CHUNK, LANES), 1) < CHUNK
    bd_mask = (_iota((LANES, LANES), 0) // CHUNK) == (_iota((LANES, LANES), 1) // CHUNK)

    def blockdiag(m):
        zero = jnp.zeros_like(m)
        return jnp.concatenate([jnp.where(lo_lane, m, zero), jnp.where(lo_lane, zero, m)], axis=0)

    groups_per_batch = nc // GROUP_CHUNKS

    def solve_group(g, carry):
        b = g // groups_per_batch
        c0 = (g % groups_per_batch) * GROUP_CHUNKS
        units = []
        for dc in range(GROUP_CHUNKS):
            c = c0 + dc
            rws = pl.ds(pl.multiple_of(c * CHUNK, CHUNK), CHUNK)
            grows = rows_s[b, c]
            for p in range(N_PAIRS):
                sl = slice(p * LANES, (p + 1) * LANES)
                q = q_ref[b, rws, sl]
                k = k_ref[b, rws, sl]
                v = v_ref[b, rws, sl]
                bet = bg_s[b, rws, sl]
                gi = bg_s[b, rws, B_WIDTH + p * LANES:B_WIDTH + (p + 1) * LANES]
                dec = jnp.exp(jnp.minimum(gi - grows[p:p + 1, :], 0.0))
                eg = jnp.exp(gi)
                glast = gi[CHUNK - 1:CHUNK, :]
                k16 = k.astype(BF16)
                qk = _dot_nt(jnp.concatenate([q.astype(BF16), k16], axis=0), blockdiag(k16))
                nk = jnp.where(strict, qk[CHUNK:2 * CHUNK] * (-bet) * dec, 0.0)
                gt_s[b, c, p:p + 1, :] = jnp.exp(glast)
                units.append(dict(
                    c=c, p=p, rws=rws, sl=sl,
                    pm=jnp.where(incl, qk[0:CHUNK] * dec, 0.0).astype(BF16),
                    tm=eye + nk, nk=nk.astype(BF16),
                    rhs=jnp.concatenate([blockdiag((v * bet).astype(BF16)),
                                         blockdiag((k * (bet * eg)).astype(BF16))], axis=1),
                    qd=q * eg, kd=(k * jnp.exp(glast - gi)).astype(BF16)))
        for u in units:
            u["nk"] = _dot(u["nk"], blockdiag(u["nk"])).astype(BF16)
        for _ in range(4):
            for u in units:
                both = _dot(u["nk"], jnp.concatenate([blockdiag(u["nk"]), blockdiag(u["tm"].astype(BF16))], axis=1))
                u["nk"] = both[:, 0:LANES].astype(BF16)
                u["tm"] = u["tm"] + both[:, LANES:2 * LANES]
        for u in units:
            u["tm"] = u["tm"] + _dot(u["nk"], blockdiag(u["tm"].astype(BF16)))
        for u in units:
            u["sol"] = _dot(u["tm"].astype(BF16), u["rhs"]).astype(BF16)
        for u in units:
            ab = _dot_tn(u["kd"], u["sol"])
            bb_s[b, u["c"], u["p"]] = jnp.where(bd_mask, ab[:, 0:LANES], 0.0)
            qa_s[b, u["c"], u["p"], CHUNK:CHUNK + LANES, :] = jnp.where(bd_mask, -ab[:, LANES:2 * LANES], 0.0).astype(BF16)
        for u in units:
            sol = u["sol"]
            po = _dot(u["pm"], jnp.concatenate([blockdiag(sol[:, 0:LANES]), blockdiag(sol[:, LANES:2 * LANES])], axis=1))
            o_s[b, u["rws"], u["sl"]] = po[:, 0:LANES]
            qa_s[b, u["c"], u["p"], 0:CHUNK, :] = (u["qd"] - po[:, LANES:2 * LANES]).astype(BF16)
        return carry

    lax.fori_loop(0, nb * groups_per_batch, solve_group, 0)

    def scan_chunk(c, carry):
        rws = pl.ds(pl.multiple_of(c * CHUNK, CHUNK), CHUNK)
        chains = [(b, p) for b in range(nb) for p in range(N_PAIRS)]
        states = [s_ref[b, p] for b, p in chains]
        prods = [_dot(qa_s[b, c, p], s.astype(BF16)) for (b, p), s in zip(chains, states)]
        for (b, p), s, r in zip(chains, states, prods):
            sl = slice(p * LANES, (p + 1) * LANES)
            s_ref[b, p] = gt_s[b, c, p:p + 1, :] * s + r[CHUNK:CHUNK + LANES] + bb_s[b, c, p]
            o_s[b, rws, sl] = o_s[b, rws, sl] + r[0:CHUNK]
        return carry

    lax.fori_loop(0, nc, scan_chunk, 0)

    for b in range(nb):
        for p in range(N_PAIRS):
            sl = slice(p * LANES, (p + 1) * LANES)
            o = o_s[b, :, sl]
            ss = _group_sumsq(o, ones2)
            o_ref[b, :, sl] = (o * lax.rsqrt(ss * (1.0 / 64.0) + EPS) * ng_ref[...]
                               * _silu(z_ref[b, :, sl])).astype(o_ref.dtype)


def _chunk_triangles(tb):
    r = np.arange(tb)[:, None]
    c = np.arange(tb)[None, :]
    same = (r // CHUNK) == (c // CHUNK)
    c_sw = np.where((c % LANES) < CHUNK, c + CHUNK, c - CHUNK)
    tril = same & (c <= r)
    triu = same & (r <= c)
    triu_sw = ((r // CHUNK) == (c_sw // CHUNK)) & (r <= c_sw)
    return jnp.asarray(np.stack([tril, triu, triu_sw]).astype(np.float32), BF16)


def _gdn(h3, arow, dtrow, ebg, ng, l):
    batch, seq, _ = h3.shape
    tb = min(GDN_TB, seq)
    nc = tb // CHUNK
    hspec = lambda width, col: pl.BlockSpec((batch, tb, width), lambda t: (0, t, col))
    return pl.pallas_call(
        _gdn_kernel,
        grid=(seq // tb,),
        in_specs=[
            hspec(B_WIDTH, COL_BQ), hspec(B_WIDTH, COL_BK), hspec(B_WIDTH, COL_BV), hspec(B_WIDTH, COL_BZ),
            hspec(GATE_W, COL_G),
            _layer_spec((1, GATE_W), l),
            _layer_spec((1, GATE_W), l),
            pl.BlockSpec((2 * GATE_W, 2 * B_WIDTH), lambda t: (0, 0)),
            pl.BlockSpec((3, tb, tb), lambda t: (0, 0, 0)),
            _layer_spec((1, LANES), l),
        ],
        out_specs=pl.BlockSpec((batch, tb, B_WIDTH), lambda t: (0, t, 0)),
        out_shape=jax.ShapeDtypeStruct((batch, seq, B_WIDTH), BF16),
        scratch_shapes=[
            pltpu.VMEM((batch, tb, 2 * B_WIDTH), F32),
            pltpu.VMEM((batch, nc, 8, LANES), F32),
            pltpu.VMEM((batch, nc, N_PAIRS, CHUNK + LANES, LANES), BF16),
            pltpu.VMEM((batch, nc, N_PAIRS, LANES, LANES), F32),
            pltpu.VMEM((batch, nc, 8, LANES), F32),
            pltpu.VMEM((batch, tb, B_WIDTH), F32),
            pltpu.VMEM((batch, N_PAIRS, LANES, LANES), F32),
        ],
        compiler_params=pltpu.CompilerParams(dimension_semantics=("arbitrary",), vmem_limit_bytes=VMEM_LIMIT),
        name="gdn",
    )(h3, h3, h3, h3, h3, arow, dtrow, ebg, _chunk_triangles(tb), ng)


def _out_ffn_kernel(l, x_ref, a_ref, b_ref, c_ref, wo_hbm, g2_ref, b2_ref, w1_hbm, w3_hbm, w2_hbm, g3_ref, b3_ref,
                    o_ref, act_ref, w1_ref, w3_ref, w2_ref, col_stage, row_stage, wo_ref, sem_o, sem):
    first_step = pl.program_id(0) == 0

    @pl.when(first_step)
    def _():
        _stream_in([(wo_hbm.at[l, pl.ds(j * FFN_TF, FFN_TF), :], row_stage, 0,
                     _store_cast(wo_ref, slice(j * FFN_TF, (j + 1) * FFN_TF), slice(None)))
                    for j in range(D_MODEL // FFN_TF)], sem_o)

    for rows in _row_slabs(x_ref.shape[0]):
        mix = _dot(a_ref[rows, :], wo_ref[0:A_WIDTH, :])
        mix = mix + _dot(b_ref[rows, :], wo_ref[A_WIDTH:A_WIDTH + B_WIDTH, :])
        mix = mix + _dot(c_ref[rows, :], wo_ref[A_WIDTH + B_WIDTH:, :])
        o_ref[rows, :] = _layer_norm(ALPHA * x_ref[rows, :] + mix, g2_ref[...], b2_ref[...])
    _ffn_from_hbm(l, first_step, o_ref, w1_hbm, w3_hbm, w2_hbm, g3_ref, b3_ref, o_ref,
                  act_ref, w1_ref, w3_ref, w2_ref, col_stage, row_stage, sem)


def _out_ffn(x, a, b, c, wo, g2, b2, w1, w3, w2, g3, b3, l):
    n = x.shape[0]
    tm = min(FFN_TM, n)
    rows = lambda width: pl.BlockSpec((tm, width), lambda i: (i, 0))
    return pl.pallas_call(
        functools.partial(_out_ffn_kernel, l),
        grid=(n // tm,),
        in_specs=[
            rows(D_MODEL), rows(A_WIDTH), rows(B_WIDTH), rows(C_WIDTH),
            HBM,
            _layer_spec((1, D_MODEL), l),
            _layer_spec((1, D_MODEL), l),
            HBM, HBM, HBM,
            _layer_spec((1, D_MODEL), l),
            _layer_spec((1, D_MODEL), l),
        ],
        out_specs=rows(D_MODEL),
        out_shape=jax.ShapeDtypeStruct((n, D_MODEL), F32),
        scratch_shapes=_ffn_scratch(tm) + [pltpu.VMEM((D_MODEL, D_MODEL), BF16),
                                           pltpu.SemaphoreType.DMA((1, 2)),
                                           pltpu.SemaphoreType.DMA((2, 2))],
        compiler_params=pltpu.CompilerParams(dimension_semantics=("arbitrary",), vmem_limit_bytes=VMEM_LIMIT),
        name="out_ffn",
    )(x, a, b, c, wo, g2, b2, w1, w3, w2, g3, b3)


def _bcast_selector():
    e = np.zeros((2 * GATE_W, 2 * B_WIDTH), np.float32)
    for j in range(B_WIDTH):
        hd = j // 64
        for base in (0, GATE_W):
            e[base + hd, j] = 1.0
            e[base + B_HEADS + hd, B_WIDTH + j] = 1.0
    return jnp.asarray(e, BF16)


def _gate_row(vec, offset):
    l, n = vec.shape
    return jnp.zeros((l, 1, GATE_W), F32).at[:, 0, offset:offset + n].set(vec)


def kernel(x, ffn1_w1, ffn1_w3, ffn1_w2, ln1_g, ln1_b, w_in, sgu_ln_g, sgu_ln_b, sgu_w, sgu_b, gdn_conv_w, gdn_a_log, gdn_dt_bias, gdn_norm_g, gla_gate_up, gla_gate_b, gla_norm_g, w_out, ln2_g, ln2_b, ffn2_w1, ffn2_w3, ffn2_w2, ln3_g, ln3_b):
    batch, seq, d = x.shape
    depth = w_in.shape[0]
    row = lambda p: p[:, None, :]
    f1 = (ffn1_w1, ffn1_w3, ffn1_w2)
    f2 = (ffn2_w1, ffn2_w3, ffn2_w2)
    w_in_b = w_in.astype(BF16)
    sgu_bias = jnp.repeat(jnp.swapaxes(sgu_b, 1, 2), HEAD_DIM, axis=2)
    arow = _gate_row(gdn_a_log, B_HEADS)
    dtrow = _gate_row(gdn_dt_bias, B_HEADS)
    ebg = _bcast_selector()
    gdn_ng = jnp.tile(gdn_norm_g, (1, 2))[:, None, :]
    gla_ng = jnp.tile(gla_norm_g, (1, C_HEADS))[:, None, :]
    gup = jnp.zeros((depth, GATE_W, C_KEY), F32).at[:, 2 * B_HEADS:2 * B_HEADS + C_RANK, :].set(gla_gate_up)

    xf = x.reshape(batch * seq, d)
    for l in range(depth):
        xf = _ffn_ln(xf, *f1, row(ln1_g), row(ln1_b), l)
        h, out_a = _proj_sgu(xf, w_in_b, row(sgu_ln_g), row(sgu_ln_b), sgu_w, sgu_bias, gdn_conv_w, seq, l)
        h3 = h.reshape(batch, seq, D_H)
        out_b = _gdn(h3, arow, dtrow, ebg, gdn_ng, l).reshape(batch * seq, B_WIDTH)
        out_c = _gla(h3, gup, row(gla_gate_b), gla_ng, l).reshape(batch * seq, C_WIDTH)
        xf = _out_ffn(xf, out_a, out_b, out_c, w_out, row(ln2_g), row(ln2_b), *f2, row(ln3_g), row(ln3_b), l)
    return xf.reshape(batch, seq, d)
```

```python
import functools

import numpy as np
import jax
import jax.numpy as jnp
from jax import lax
from jax.experimental import pallas as pl
from jax.experimental.pallas import tpu as pltpu

F32 = jnp.float32
BF16 = jnp.bfloat16
HIGHEST = lax.Precision.HIGHEST

D_MODEL = 1024
DEPTH = 4
HEAD_DIM = 64
A_HEADS = 4
A_WIDTH = A_HEADS * HEAD_DIM
A_CHUNK = 128
B_HEADS = 8
B_DK = 64
B_WIDTH = B_HEADS * 64
B_CONV = 4
C_HEADS = 4
C_DK = 32
C_DV = 64
C_KEY = C_HEADS * C_DK
C_WIDTH = C_HEADS * C_DV
C_RANK = 16
C_TAU = 16.0
D_FF = 2816
ALPHA = (2.0 * DEPTH) ** 0.25
EPS = 1e-5

LANES = 128
CHUNK = 64
GATE_W = LANES
N_PAIRS = B_HEADS // 2
TAIL = 8
D_IN_PAD = 2 * A_WIDTH + 4 * B_WIDTH + 2 * C_WIDTH + 2 * C_KEY + GATE_W
VMEM_LIMIT = 56 * 1024 * 1024

D_H = D_IN_PAD - 2 * A_WIDTH

COL_BQ, COL_BK, COL_BV, COL_BZ = 0, 1, 2, 3
COL_CV, COL_CR = 8, 9
COL_CQ, COL_CK, COL_G = 20, 21, 22


def _dot(a, b, **kw):
    return jnp.dot(a, b, preferred_element_type=F32, **kw)


def _dot_nt(a, b, **kw):
    return lax.dot_general(a, b, (((1,), (1,)), ((), ())), preferred_element_type=F32, **kw)


def _dot_tn(a, b, **kw):
    return lax.dot_general(a, b, (((0,), (0,)), ((), ())), preferred_element_type=F32, **kw)


def _split2(x):
    hi = x.astype(BF16)
    lo = (x - hi.astype(F32)).astype(BF16)
    return jnp.concatenate([hi, lo], axis=-1)


def _split3(x, axis):
    hi = x.astype(BF16)
    r1 = x - hi.astype(F32)
    mid = r1.astype(BF16)
    lo = (r1 - mid.astype(F32)).astype(BF16)
    return jnp.concatenate([hi, mid, lo], axis=axis)


def _iota(shape, dim):
    return lax.broadcasted_iota(jnp.int32, shape, dim)


def _group_ones(width, group):
    r = _iota((width, width), 0)
    c = _iota((width, width), 1)
    return jnp.where((r // group) == (c // group), 1.0, 0.0).astype(BF16)


def _group_sumsq(x, ones):
    return _dot((x * x).astype(BF16), ones)


def _layer_norm(y, g, b):
    mu = jnp.mean(y, axis=-1, keepdims=True)
    yc = y - mu
    var = jnp.mean(yc * yc, axis=-1, keepdims=True)
    return yc * lax.rsqrt(var + EPS) * g + b


def _silu(x):
    return x * jax.nn.sigmoid(x)


def _gelu_tanh(x):
    return 0.5 * x * (1.0 + jnp.tanh(0.7978845608028654 * (x + 0.044715 * (x * x * x))))


def _softplus(x):
    return jnp.maximum(x, 0.0) + jnp.log(1.0 + jnp.exp(-jnp.abs(x)))


FFN_TM = 512
FFN_TF = 256
FFN_SLABS = 2
STREAM_DEPTH = 4


N_FF_CHUNKS = D_FF // FFN_TF


def _row_slabs(tm):
    step = tm // FFN_SLABS
    return [slice(r * step, (r + 1) * step) for r in range(FFN_SLABS)]


def _ffn_ln_rows(x, rows, w1b, w3b, w2b, g_ref, b_ref, a_ref):
    xb = x.astype(BF16)
    for j in range(N_FF_CHUNKS):
        sl = slice(j * FFN_TF, (j + 1) * FFN_TF)
        h1 = _dot(xb, w1b[:, sl])
        h3 = _dot(xb, w3b[:, sl])
        a_ref[rows, sl] = (_silu(h1) * h3).astype(BF16)
    y = ALPHA * x + 0.5 * _dot(a_ref[rows, :], w2b[...])
    return _layer_norm(y, g_ref[...], b_ref[...])


def _stream_in(plan, sem):
    used = {}
    steps = []
    for src, stage, sid, consume in plan:
        slot = used.get(sid, 0) % STREAM_DEPTH
        used[sid] = used.get(sid, 0) + 1
        steps.append((pltpu.make_async_copy(src, stage.at[slot], sem.at[sid, slot]), stage, slot, consume))
    ahead = STREAM_DEPTH - 1
    for k in range(min(ahead, len(steps))):
        steps[k][0].start()
    for k, (copy, stage, slot, consume) in enumerate(steps):
        if k + ahead < len(steps):
            steps[k + ahead][0].start()
        copy.wait()
        consume(stage[slot])


def _store_cast(dst, rows, cols):
    def consume(v):
        dst[rows, cols] = v.astype(BF16)
    return consume


def _ffn_weight_plan(l, w1_hbm, w3_hbm, w2_hbm, w1b, w3b, w2b, col_stage, row_stage):
    plan = []
    full = slice(None)
    for j in range(N_FF_CHUNKS):
        win = pl.ds(j * FFN_TF, FFN_TF)
        sl = slice(j * FFN_TF, (j + 1) * FFN_TF)
        plan.append((w1_hbm.at[l, :, win], col_stage, 0, _store_cast(w1b, full, sl)))
        plan.append((w3_hbm.at[l, :, win], col_stage, 0, _store_cast(w3b, full, sl)))
        plan.append((w2_hbm.at[l, win, :], row_stage, 1, _store_cast(w2b, sl, full)))
    return plan


def _ffn_scratch(tm):
    return [
        pltpu.VMEM((tm, D_FF), BF16),
        pltpu.VMEM((D_MODEL, D_FF), BF16),
        pltpu.VMEM((D_MODEL, D_FF), BF16),
        pltpu.VMEM((D_FF, D_MODEL), BF16),
        pltpu.VMEM((STREAM_DEPTH, D_MODEL, FFN_TF), F32),
        pltpu.VMEM((STREAM_DEPTH, FFN_TF, D_MODEL), F32),
    ]


def _ffn_ln_kernel(l, x_ref, w1_hbm, w3_hbm, w2_hbm, g_ref, b_ref, o_ref,
                   a_ref, w1b, w3b, w2b, col_stage, row_stage, sem):
    @pl.when(pl.program_id(0) == 0)
    def _():
        _stream_in(_ffn_weight_plan(l, w1_hbm, w3_hbm, w2_hbm, w1b, w3b, w2b, col_stage, row_stage), sem)

    for rows in _row_slabs(x_ref.shape[0]):
        o_ref[rows, :] = _ffn_ln_rows(x_ref[rows, :], rows, w1b, w3b, w2b, g_ref, b_ref, a_ref)


def _layer_spec(shape, l, **kw):
    zeros = (0,) * len(shape)
    return pl.BlockSpec((None,) + tuple(shape), lambda *_: (l,) + zeros, **kw)


HBM = pl.BlockSpec(memory_space=pl.ANY)


def _ffn_ln(x, w1, w3, w2, g, b, l):
    n = x.shape[0]
    tm = min(FFN_TM, n)
    return pl.pallas_call(
        functools.partial(_ffn_ln_kernel, l),
        grid=(n // tm,),
        in_specs=[
            pl.BlockSpec((tm, D_MODEL), lambda i: (i, 0)),
            HBM, HBM, HBM,
            _layer_spec((1, D_MODEL), l),
            _layer_spec((1, D_MODEL), l),
        ],
        out_specs=pl.BlockSpec((tm, D_MODEL), lambda i: (i, 0)),
        out_shape=jax.ShapeDtypeStruct((n, D_MODEL), F32),
        scratch_shapes=_ffn_scratch(tm) + [pltpu.SemaphoreType.DMA((2, STREAM_DEPTH))],
        compiler_params=pltpu.CompilerParams(dimension_semantics=("arbitrary",), vmem_limit_bytes=VMEM_LIMIT),
        name="ffn_ln",
    )(x, w1, w3, w2, g, b)


PROJ_TM = 512
PROJ_WR = 128
D_IN = 2 * A_WIDTH + 4 * B_WIDTH + 2 * B_HEADS + 2 * C_KEY + 2 * C_WIDTH + C_RANK


def _permute_w_in(v):
    o = np.cumsum([0, A_WIDTH, A_WIDTH, 3 * B_WIDTH, B_WIDTH, B_HEADS, B_HEADS, C_KEY, C_KEY, C_WIDTH, C_WIDTH, C_RANK])
    front = v[:, 0:o[4]]
    gates_b = v[:, o[4]:o[6]]
    c_qk = v[:, o[6]:o[8]]
    c_vr = v[:, o[8]:o[10]]
    c_g = v[:, o[10]:o[11]]
    pad = jnp.zeros((v.shape[0], GATE_W - 2 * B_HEADS - C_RANK), F32)
    return jnp.concatenate([front, c_vr, c_qk, gates_b, c_g, pad], axis=-1).astype(BF16)


def _proj_sgu_kernel(tiles_per_seq, l, x_ref, w_hbm, lng_ref, lnb_ref, sw_ref, bias_ref, cw_ref, h_ref, a_ref,
                     xbuf, w_ref, w_stage, sem):
    tm = x_ref.shape[0]
    w3 = 3 * B_WIDTH
    c0 = 2 * A_WIDTH

    @pl.when(pl.program_id(0) == 0)
    def _():
        def store_rows(r):
            def consume(v):
                w_ref[r * PROJ_WR:(r + 1) * PROJ_WR, :] = _permute_w_in(v.astype(F32))
            return consume
        _stream_in([(w_hbm.at[l, pl.ds(r * PROJ_WR, PROJ_WR), :], w_stage, 0, store_rows(r))
                    for r in range(D_MODEL // PROJ_WR)], sem)

    @pl.when(pl.program_id(0) % tiles_per_seq == 0)
    def _():
        xbuf[0:TAIL, :] = jnp.zeros((TAIL, w3), F32)

    xb = x_ref[...].astype(BF16)
    xbuf[TAIL:TAIL + tm, :] = _dot(xb, w_ref[:, c0:c0 + w3])
    y = cw_ref[B_CONV - 1:B_CONV, :] * xbuf[TAIL:TAIL + tm, :]
    for i in range(B_CONV - 1):
        back = B_CONV - 1 - i
        y = y + cw_ref[i:i + 1, :] * xbuf[TAIL - back:TAIL - back + tm, :]
    xbuf[0:TAIL, :] = xbuf[tm:tm + TAIL, :]
    y = _silu(y)
    h_ref[:, 2 * B_WIDTH:w3] = y[:, 2 * B_WIDTH:w3]

    h_ref[:, w3:] = _dot(xb, w_ref[:, c0 + w3:])
    ha = _dot(xb, w_ref[:, 0:c0])
    u = _gelu_tanh(ha[:, 0:A_WIDTH])
    v = _layer_norm(_gelu_tanh(ha[:, A_WIDTH:2 * A_WIDTH]), lng_ref[...], lnb_ref[...])
    tril = _iota((A_CHUNK, A_CHUNK), 1) <= _iota((A_CHUNK, A_CHUNK), 0)
    head_of_lane = _iota((A_CHUNK, A_WIDTH), 1) // HEAD_DIM
    wts = [jnp.where(tril, sw_ref[h], 0.0).astype(BF16) for h in range(A_HEADS)]
    bias = bias_ref[...]
    for c in range(tm // A_CHUNK):
        rows = slice(c * A_CHUNK, (c + 1) * A_CHUNK)
        vc = v[rows]
        z = bias
        for h in range(A_HEADS):
            z = z + _dot(wts[h], jnp.where(head_of_lane == h, vc, 0.0).astype(BF16))
        a_ref[rows, :] = (u[rows] * z).astype(a_ref.dtype)

    ones2 = _group_ones(LANES, B_DK)
    for p in range(N_PAIRS):
        sl = slice(p * LANES, (p + 1) * LANES)
        yq = y[:, sl]
        h_ref[:, sl] = yq * (lax.rsqrt(_group_sumsq(yq, ones2) + 1e-6) * (B_DK ** -0.5))
        ksl = slice(B_WIDTH + p * LANES, B_WIDTH + (p + 1) * LANES)
        yk = y[:, ksl]
        h_ref[:, ksl] = yk * lax.rsqrt(_group_sumsq(yk, ones2) + 1e-6)


def _proj_sgu(x, w, lng, lnb, sw, bias, cw, seq, l):
    n = x.shape[0]
    tm = min(PROJ_TM, seq)
    return pl.pallas_call(
        functools.partial(_proj_sgu_kernel, seq // tm, l),
        grid=(n // tm,),
        in_specs=[
            pl.BlockSpec((tm, D_MODEL), lambda i: (i, 0)),
            HBM,
            _layer_spec((1, A_WIDTH), l),
            _layer_spec((1, A_WIDTH), l),
            _layer_spec((A_HEADS, A_CHUNK, A_CHUNK), l),
            _layer_spec((A_CHUNK, A_WIDTH), l),
            _layer_spec((B_CONV, 3 * B_WIDTH), l),
        ],
        out_specs=[pl.BlockSpec((tm, D_H), lambda i: (i, 0)), pl.BlockSpec((tm, A_WIDTH), lambda i: (i, 0))],
        out_shape=[jax.ShapeDtypeStruct((n, D_H), F32), jax.ShapeDtypeStruct((n, A_WIDTH), BF16)],
        scratch_shapes=[
            pltpu.VMEM((tm + TAIL, 3 * B_WIDTH), F32),
            pltpu.VMEM((D_MODEL, D_IN_PAD), BF16),
            pltpu.VMEM((STREAM_DEPTH, PROJ_WR, D_IN), BF16),
            pltpu.SemaphoreType.DMA((1, STREAM_DEPTH)),
        ],
        compiler_params=pltpu.CompilerParams(dimension_semantics=("arbitrary",), vmem_limit_bytes=VMEM_LIMIT),
        name="proj_sgu",
    )(x, w, lng, lnb, sw, bias, cw)


GLA_TB = 512


def _gla_kernel(q_ref, k_ref, v_ref, r_ref, g_ref, gup_ref, gb_ref, tri_ref, ng_ref, o_ref, st_ref, ob_ref):
    nb, tb = q_ref.shape[0], q_ref.shape[1]
    nc = tb // CHUNK

    @pl.when(pl.program_id(0) == 0)
    def _():
        st_ref[...] = jnp.zeros_like(st_ref)

    khead = _iota((CHUNK, C_KEY), 1) // C_DK
    vhead = _iota((CHUNK, C_WIDTH), 1) // C_DV
    causal = (_iota((CHUNK, C_WIDTH), 1) % CHUNK) <= _iota((CHUNK, C_WIDTH), 0)
    st_mask = (_iota((C_WIDTH, C_KEY), 0) // C_DV) == (_iota((C_WIDTH, C_KEY), 1) // C_DK)

    units = []
    for b in range(nb):
        logit = _dot(g_ref[b], gup_ref[...], precision=HIGHEST) + gb_ref[...]
        log_a = -_softplus(-logit) * (1.0 / C_TAU)
        g3 = _dot(tri_ref[...], _split3(log_a, 1))
        bc_all = g3[:, 0:C_KEY] + g3[:, C_KEY:2 * C_KEY] + g3[:, 2 * C_KEY:3 * C_KEY]
        for c in range(nc):
            rows = slice(c * CHUNK, (c + 1) * CHUNK)
            bc = bc_all[rows]
            btot = bc[CHUNK - 1:CHUNK, :]
            k = k_ref[b, rows, :]
            kn = k * jnp.exp(-bc)
            vb = v_ref[b, rows, :].astype(BF16)
            units.append(dict(
                b=b, rows=rows, vb=vb, gt=jnp.exp(btot),
                qd=(q_ref[b, rows, :] * ((C_DK ** -0.5) * jnp.exp(bc))).astype(BF16),
                kd=(k * jnp.exp(btot - bc)).astype(BF16),
                kstack=jnp.concatenate([jnp.where(khead == h, kn, 0.0) for h in range(C_HEADS)], axis=0).astype(BF16),
                vbd=jnp.concatenate([jnp.where(vhead == h, vb, jnp.zeros_like(vb)) for h in range(C_HEADS)], axis=0)))
    for u in units:
        u["p"] = jnp.where(causal, _dot_nt(u["qd"], u["kstack"]), 0.0).astype(BF16)
    for u in units:
        u["upd"] = jnp.where(st_mask, _dot_tn(u["vb"], u["kd"]), 0.0)
    for u in units:
        u["oi"] = _dot(u["p"], u["vbd"])

    states = [st_ref[b] for b in range(nb)]
    for u in units:
        b = u["b"]
        ob_ref[b, u["rows"], :] = u["oi"] + _dot_nt(u["qd"], states[b].astype(BF16))
        states[b] = u["gt"] * states[b] + u["upd"]
    for b in range(nb):
        st_ref[b] = states[b]

    ones = _group_ones(C_WIDTH, C_DV)
    for b in range(nb):
        o = ob_ref[b]
        ss = _group_sumsq(o, ones)
        o_ref[b] = (o * lax.rsqrt(ss * (1.0 / C_DV) + EPS) * ng_ref[...] * _silu(r_ref[b])).astype(o_ref.dtype)


def _gla(h3, gup, gb, ng, l):
    batch, seq, _ = h3.shape
    tb = min(GLA_TB, seq)
    hspec = lambda width, col: pl.BlockSpec((batch, tb, width), lambda t: (0, t, col))
    return pl.pallas_call(
        _gla_kernel,
        grid=(seq // tb,),
        in_specs=[
            hspec(C_KEY, COL_CQ), hspec(C_KEY, COL_CK), hspec(C_WIDTH, COL_CV), hspec(C_WIDTH, COL_CR),
            hspec(GATE_W, COL_G),
            _layer_spec((GATE_W, C_KEY), l),
            _layer_spec((1, C_KEY), l),
            pl.BlockSpec((tb, tb), lambda t: (0, 0)),
            _layer_spec((1, C_WIDTH), l),
        ],
        out_specs=pl.BlockSpec((batch, tb, C_WIDTH), lambda t: (0, t, 0)),
        out_shape=jax.ShapeDtypeStruct((batch, seq, C_WIDTH), BF16),
        scratch_shapes=[pltpu.VMEM((batch, C_WIDTH, C_KEY), F32), pltpu.VMEM((batch, tb, C_WIDTH), F32)],
        compiler_params=pltpu.CompilerParams(dimension_semantics=("arbitrary",), vmem_limit_bytes=VMEM_LIMIT),
        name="gla",
    )(h3, h3, h3, h3, h3, gup, gb, _chunk_triangles(tb)[0], ng)


GDN_TB = 256
GROUP_CHUNKS = 4


def _gdn_kernel(q_ref, k_ref, v_ref, z_ref, g_ref, arow_ref, dtrow_ref, ebg_ref, tri_ref, ng_ref, o_ref,
                bg_s, rows_s, qa_s, bb_s, gt_s, o_s, s_ref):
    nb, tb = q_ref.shape[0], q_ref.shape[1]
    nc = tb // CHUNK

    @pl.when(pl.program_id(0) == 0)
    def _():
        s_ref[...] = jnp.zeros_like(s_ref)

    ones2 = _group_ones(LANES, B_DK)
    lane = _iota((tb, GATE_W), 1)
    low_half = _iota((1, LANES), 1) < CHUNK
    for b in range(nb):
        gates = g_ref[b]
        beta = jax.nn.sigmoid(gates)
        lg = jnp.where((lane >= B_HEADS) & (lane < 2 * B_HEADS),
                       -jnp.exp(arow_ref[...]) * _softplus(gates + dtrow_ref[...]), 0.0)
        g3 = _dot(tri_ref[0], _split3(lg, 1))
        gam = g3[:, 0:GATE_W] + g3[:, GATE_W:2 * GATE_W] + g3[:, 2 * GATE_W:3 * GATE_W]
        feat = jnp.where(lane < B_HEADS, beta, gam)
        bg_s[b] = _dot(_split2(feat), ebg_ref[...])

        lg_t3 = _split3(lg.T[0:2 * B_HEADS, :], 0)
        nh2 = 2 * B_HEADS
        t3 = _dot(lg_t3, tri_ref[1])
        gam_t = t3[0:nh2] + t3[nh2:2 * nh2] + t3[2 * nh2:3 * nh2]
        t3 = _dot(lg_t3, tri_ref[2])
        gam_t_sw = t3[0:nh2] + t3[nh2:2 * nh2] + t3[2 * nh2:3 * nh2]
        for c in range(nc):
            tile = slice((c // 2) * LANES, (c // 2 + 1) * LANES)
            first, second = (gam_t, gam_t_sw) if c % 2 == 0 else (gam_t_sw, gam_t)
            for p in range(N_PAIRS):
                h0 = B_HEADS + 2 * p
                rows_s[b, c, p:p + 1, :] = jnp.where(low_half, first[h0:h0 + 1, tile], second[h0 + 1:h0 + 2, tile])

    ii = _iota((CHUNK, LANES), 0)
    jj = _iota((CHUNK, LANES), 1) % CHUNK
    incl = jj <= ii
    strict = jj < ii
    eye = jnp.where(jj == ii, 1.0, 0.0)
    lo_lane = _iota((CHUNK, LANES), 1) < CHUNK

    def blockdiag(m):
        zero = jnp.zeros_like(m)
        return jnp.concatenate([jnp.where(lo_lane, m, zero), jnp.where(lo_lane, zero, m)], axis=0)

    groups_per_batch = nc // GROUP_CHUNKS

    def solve_group(g, carry):
        b = g // groups_per_batch
        c0 = (g % groups_per_batch) * GROUP_CHUNKS
        units = []
        for dc in range(GROUP_CHUNKS):
            c = c0 + dc
            rws = pl.ds(pl.multiple_of(c * CHUNK, CHUNK), CHUNK)
            grows = rows_s[b, c]
            for p in range(N_PAIRS):
                sl = slice(p * LANES, (p + 1) * LANES)
                q = q_ref[b, rws, sl]
                k = k_ref[b, rws, sl]
                v = v_ref[b, rws, sl]
                bet = bg_s[b, rws, sl]
                gi = bg_s[b, rws, B_WIDTH + p * LANES:B_WIDTH + (p + 1) * LANES]
                dec = jnp.exp(jnp.minimum(gi - grows[p:p + 1, :], 0.0))
                eg = jnp.exp(gi)
                glast = gi[CHUNK - 1:CHUNK, :]
                k16 = k.astype(BF16)
                qk = _dot_nt(jnp.concatenate([q.astype(BF16), k16], axis=0), blockdiag(k16))
                nk = jnp.where(strict, qk[CHUNK:2 * CHUNK] * (-bet) * dec, 0.0)
                gt_s[b, c, p:p + 1, :] = jnp.exp(glast)
                units.append(dict(
                    c=c, p=p, rws=rws, sl=sl,
                    pm=jnp.where(incl, qk[0:CHUNK] * dec, 0.0).astype(BF16),
                    tm=eye + nk, nk=nk.astype(BF16),
                    rhs=jnp.concatenate([blockdiag((v * bet).astype(BF16)),
                                         blockdiag((k * (bet * eg)).astype(BF16))], axis=1),
                    qd=q * eg, kd=(k * jnp.exp(glast - gi)).astype(BF16)))
        for u in units:
            u["nk"] = _dot(u["nk"], blockdiag(u["nk"])).astype(BF16)
        for _ in range(4):
            for u in units:
                both = _dot(u["nk"], jnp.concatenate([blockdiag(u["nk"]), blockdiag(u["tm"].astype(BF16))], axis=1))
                u["nk"] = both[:, 0:LANES].astype(BF16)
                u["tm"] = u["tm"] + both[:, LANES:2 * LANES]
        for u in units:
            u["tm"] = u["tm"] + _dot(u["nk"], blockdiag(u["tm"].astype(BF16)))
        for u in units:
            u["sol"] = _dot(u["tm"].astype(BF16), u["rhs"]).astype(BF16)
        for u in units:
            ab = _dot_tn(u["kd"], u["sol"])
            bb_s[b, u["c"], u["p"]] = jnp.where(lo_lane, ab[0:CHUNK, 0:LANES], ab[CHUNK:2 * CHUNK, 0:LANES])
            qa_s[b, u["c"], u["p"], CHUNK:2 * CHUNK, :] = (
                -jnp.where(lo_lane, ab[0:CHUNK, LANES:2 * LANES], ab[CHUNK:2 * CHUNK, LANES:2 * LANES])).astype(BF16)
        for u in units:
            sol = u["sol"]
            po = _dot(u["pm"], jnp.concatenate([blockdiag(sol[:, 0:LANES]), blockdiag(sol[:, LANES:2 * LANES])], axis=1))
            o_s[b, u["rws"], u["sl"]] = po[:, 0:LANES]
            qa_s[b, u["c"], u["p"], 0:CHUNK, :] = (u["qd"] - po[:, LANES:2 * LANES]).astype(BF16)
        return carry

    lax.fori_loop(0, nb * groups_per_batch, solve_group, 0)

    def scan_chunk(c, carry):
        rws = pl.ds(pl.multiple_of(c * CHUNK, CHUNK), CHUNK)
        chains = [(b, p) for b in range(nb) for p in range(N_PAIRS)]
        states = [s_ref[b, p] for b, p in chains]
        prods = [_dot(qa_s[b, c, p], blockdiag(s.astype(BF16))) for (b, p), s in zip(chains, states)]
        for (b, p), s, r in zip(chains, states, prods):
            sl = slice(p * LANES, (p + 1) * LANES)
            s_ref[b, p] = gt_s[b, c, p:p + 1, :] * s + r[CHUNK:2 * CHUNK] + bb_s[b, c, p]
            o_s[b, rws, sl] = o_s[b, rws, sl] + r[0:CHUNK]
        return carry

    lax.fori_loop(0, nc, scan_chunk, 0)

    for b in range(nb):
        for p in range(N_PAIRS):
            sl = slice(p * LANES, (p + 1) * LANES)
            o = o_s[b, :, sl]
            ss = _group_sumsq(o, ones2)
            o_ref[b, :, sl] = (o * lax.rsqrt(ss * (1.0 / 64.0) + EPS) * ng_ref[...]
                               * _silu(z_ref[b, :, sl])).astype(o_ref.dtype)


def _chunk_triangles(tb):
    r = np.arange(tb)[:, None]
    c = np.arange(tb)[None, :]
    same = (r // CHUNK) == (c // CHUNK)
    c_sw = np.where((c % LANES) < CHUNK, c + CHUNK, c - CHUNK)
    tril = same & (c <= r)
    triu = same & (r <= c)
    triu_sw = ((r // CHUNK) == (c_sw // CHUNK)) & (r <= c_sw)
    return jnp.asarray(np.stack([tril, triu, triu_sw]).astype(np.float32), BF16)


def _gdn(h3, arow, dtrow, ebg, ng, l):
    batch, seq, _ = h3.shape
    tb = min(GDN_TB, seq)
    nc = tb // CHUNK
    hspec = lambda width, col: pl.BlockSpec((batch, tb, width), lambda t: (0, t, col))
    return pl.pallas_call(
        _gdn_kernel,
        grid=(seq // tb,),
        in_specs=[
            hspec(B_WIDTH, COL_BQ), hspec(B_WIDTH, COL_BK), hspec(B_WIDTH, COL_BV), hspec(B_WIDTH, COL_BZ),
            hspec(GATE_W, COL_G),
            _layer_spec((1, GATE_W), l),
            _layer_spec((1, GATE_W), l),
            pl.BlockSpec((2 * GATE_W, 2 * B_WIDTH), lambda t: (0, 0)),
            pl.BlockSpec((3, tb, tb), lambda t: (0, 0, 0)),
            _layer_spec((1, LANES), l),
        ],
        out_specs=pl.BlockSpec((batch, tb, B_WIDTH), lambda t: (0, t, 0)),
        out_shape=jax.ShapeDtypeStruct((batch, seq, B_WIDTH), BF16),
        scratch_shapes=[
            pltpu.VMEM((batch, tb, 2 * B_WIDTH), F32),
            pltpu.VMEM((batch, nc, 8, LANES), F32),
            pltpu.VMEM((batch, nc, N_PAIRS, 2 * CHUNK, LANES), BF16),
            pltpu.VMEM((batch, nc, N_PAIRS, CHUNK, LANES), F32),
            pltpu.VMEM((batch, nc, 8, LANES), F32),
            pltpu.VMEM((batch, tb, B_WIDTH), F32),
            pltpu.VMEM((batch, N_PAIRS, CHUNK, LANES), F32),
        ],
        compiler_params=pltpu.CompilerParams(dimension_semantics=("arbitrary",), vmem_limit_bytes=VMEM_LIMIT),
        name="gdn",
    )(h3, h3, h3, h3, h3, arow, dtrow, ebg, _chunk_triangles(tb), ng)


def _out_ffn_kernel(l, x_ref, a_ref, b_ref, c_ref, wo_hbm, g2_ref, b2_ref, w1_hbm, w3_hbm, w2_hbm, g3_ref, b3_ref,
                    o_ref, act_ref, w1_ref, w3_ref, w2_ref, col_stage, row_stage, wo_ref, sem):
    @pl.when(pl.program_id(0) == 0)
    def _():
        plan = _ffn_weight_plan(l, w1_hbm, w3_hbm, w2_hbm, w1_ref, w3_ref, w2_ref, col_stage, row_stage)
        plan += [(wo_hbm.at[l, pl.ds(j * FFN_TF, FFN_TF), :], row_stage, 1,
                  _store_cast(wo_ref, slice(j * FFN_TF, (j + 1) * FFN_TF), slice(None)))
                 for j in range(D_MODEL // FFN_TF)]
        _stream_in(plan, sem)

    slabs = _row_slabs(x_ref.shape[0])
    mixes = []
    for rows in slabs:
        mix = _dot(a_ref[rows, :], wo_ref[0:A_WIDTH, :])
        mix = mix + _dot(b_ref[rows, :], wo_ref[A_WIDTH:A_WIDTH + B_WIDTH, :])
        mixes.append(mix + _dot(c_ref[rows, :], wo_ref[A_WIDTH + B_WIDTH:, :]))
    for rows, mix in zip(slabs, mixes):
        x2 = _layer_norm(ALPHA * x_ref[rows, :] + mix, g2_ref[...], b2_ref[...])
        o_ref[rows, :] = _ffn_ln_rows(x2, rows, w1_ref, w3_ref, w2_ref, g3_ref, b3_ref, act_ref)


def _out_ffn(x, a, b, c, wo, g2, b2, w1, w3, w2, g3, b3, l):
    n = x.shape[0]
    tm = min(FFN_TM, n)
    rows = lambda width: pl.BlockSpec((tm, width), lambda i: (i, 0))
    return pl.pallas_call(
        functools.partial(_out_ffn_kernel, l),
        grid=(n // tm,),
        in_specs=[
            rows(D_MODEL), rows(A_WIDTH), rows(B_WIDTH), rows(C_WIDTH),
            HBM,
            _layer_spec((1, D_MODEL), l),
            _layer_spec((1, D_MODEL), l),
            HBM, HBM, HBM,
            _layer_spec((1, D_MODEL), l),
            _layer_spec((1, D_MODEL), l),
        ],
        out_specs=rows(D_MODEL),
        out_shape=jax.ShapeDtypeStruct((n, D_MODEL), F32),
        scratch_shapes=_ffn_scratch(tm) + [pltpu.VMEM((D_MODEL, D_MODEL), BF16),
                                           pltpu.SemaphoreType.DMA((2, STREAM_DEPTH))],
        compiler_params=pltpu.CompilerParams(dimension_semantics=("arbitrary",), vmem_limit_bytes=VMEM_LIMIT),
        name="out_ffn",
    )(x, a, b, c, wo, g2, b2, w1, w3, w2, g3, b3)


def _bcast_selector():
    e = np.zeros((2 * GATE_W, 2 * B_WIDTH), np.float32)
    for j in range(B_WIDTH):
        hd = j // 64
        for base in (0, GATE_W):
            e[base + hd, j] = 1.0
            e[base + B_HEADS + hd, B_WIDTH + j] = 1.0
    return jnp.asarray(e, BF16)


def _gate_row(vec, offset):
    l, n = vec.shape
    return jnp.zeros((l, 1, GATE_W), F32).at[:, 0, offset:offset + n].set(vec)


def kernel(x, ffn1_w1, ffn1_w3, ffn1_w2, ln1_g, ln1_b, w_in, sgu_ln_g, sgu_ln_b, sgu_w, sgu_b, gdn_conv_w, gdn_a_log, gdn_dt_bias, gdn_norm_g, gla_gate_up, gla_gate_b, gla_norm_g, w_out, ln2_g, ln2_b, ffn2_w1, ffn2_w3, ffn2_w2, ln3_g, ln3_b):
    batch, seq, d = x.shape
    depth = w_in.shape[0]
    row = lambda p: p[:, None, :]
    f1 = (ffn1_w1, ffn1_w3, ffn1_w2)
    f2 = (ffn2_w1, ffn2_w3, ffn2_w2)
    w_in_b = w_in.astype(BF16)
    sgu_bias = jnp.repeat(jnp.swapaxes(sgu_b, 1, 2), HEAD_DIM, axis=2)
    arow = _gate_row(gdn_a_log, B_HEADS)
    dtrow = _gate_row(gdn_dt_bias, B_HEADS)
    ebg = _bcast_selector()
    gdn_ng = jnp.tile(gdn_norm_g, (1, 2))[:, None, :]
    gla_ng = jnp.tile(gla_norm_g, (1, C_HEADS))[:, None, :]
    gup = jnp.zeros((depth, GATE_W, C_KEY), F32).at[:, 2 * B_HEADS:2 * B_HEADS + C_RANK, :].set(gla_gate_up)

    xf = x.reshape(batch * seq, d)
    for l in range(depth):
        xf = _ffn_ln(xf, *f1, row(ln1_g), row(ln1_b), l)
        h, out_a = _proj_sgu(xf, w_in_b, row(sgu_ln_g), row(sgu_ln_b), sgu_w, sgu_bias, gdn_conv_w, seq, l)
        h3 = h.reshape(batch, seq, D_H)
        out_b = _gdn(h3, arow, dtrow, ebg, gdn_ng, l).reshape(batch * seq, B_WIDTH)
        out_c = _gla(h3, gup, row(gla_gate_b), gla_ng, l).reshape(batch * seq, C_WIDTH)
        xf = _out_ffn(xf, out_a, out_b, out_c, w_out, row(ln2_g), row(ln2_b), *f2, row(ln3_g), row(ln3_b), l)
    return xf.reshape(batch, seq, d)
```

```python
import functools

import numpy as np
import jax
import jax.numpy as jnp
from jax import lax
from jax.experimental import pallas as pl
from jax.experimental.pallas import tpu as pltpu

F32 = jnp.float32
BF16 = jnp.bfloat16
HIGHEST = lax.Precision.HIGHEST

D_MODEL = 1024
DEPTH = 4
HEAD_DIM = 64
A_HEADS = 4
A_WIDTH = A_HEADS * HEAD_DIM
A_CHUNK = 128
B_HEADS = 8
B_DK = 64
B_WIDTH = B_HEADS * 64
B_CONV = 4
C_HEADS = 4
C_DK = 32
C_DV = 64
C_KEY = C_HEADS * C_DK
C_WIDTH = C_HEADS * C_DV
C_RANK = 16
C_TAU = 16.0
D_FF = 2816
ALPHA = (2.0 * DEPTH) ** 0.25
EPS = 1e-5

LANES = 128
CHUNK = 64
GATE_W = LANES
N_PAIRS = B_HEADS // 2
TAIL = 8
D_IN_PAD = 2 * A_WIDTH + 4 * B_WIDTH + 2 * C_WIDTH + 2 * C_KEY + GATE_W
VMEM_LIMIT = 56 * 1024 * 1024

D_H = D_IN_PAD - 2 * A_WIDTH

COL_BQ, COL_BK, COL_BV, COL_BZ = 0, 1, 2, 3
COL_CV, COL_CR = 8, 9
COL_CQ, COL_CK, COL_G = 20, 21, 22


def _dot(a, b, **kw):
    return jnp.dot(a, b, preferred_element_type=F32, **kw)


def _dot_nt(a, b, **kw):
    return lax.dot_general(a, b, (((1,), (1,)), ((), ())), preferred_element_type=F32, **kw)


def _dot_tn(a, b, **kw):
    return lax.dot_general(a, b, (((0,), (0,)), ((), ())), preferred_element_type=F32, **kw)


def _split2(x):
    hi = x.astype(BF16)
    lo = (x - hi.astype(F32)).astype(BF16)
    return jnp.concatenate([hi, lo], axis=-1)


def _split3(x, axis):
    hi = x.astype(BF16)
    r1 = x - hi.astype(F32)
    mid = r1.astype(BF16)
    lo = (r1 - mid.astype(F32)).astype(BF16)
    return jnp.concatenate([hi, mid, lo], axis=axis)


def _iota(shape, dim):
    return lax.broadcasted_iota(jnp.int32, shape, dim)


def _group_ones(width, group):
    r = _iota((width, width), 0)
    c = _iota((width, width), 1)
    return jnp.where((r // group) == (c // group), 1.0, 0.0).astype(BF16)


def _group_sumsq(x, ones):
    return _dot((x * x).astype(BF16), ones)


def _layer_norm(y, g, b):
    mu = jnp.mean(y, axis=-1, keepdims=True)
    yc = y - mu
    var = jnp.mean(yc * yc, axis=-1, keepdims=True)
    return yc * lax.rsqrt(var + EPS) * g + b


def _silu(x):
    return x * jax.nn.sigmoid(x)


def _gelu_tanh(x):
    return 0.5 * x * (1.0 + jnp.tanh(0.7978845608028654 * (x + 0.044715 * (x * x * x))))


def _softplus(x):
    return jnp.maximum(x, 0.0) + jnp.log(1.0 + jnp.exp(-jnp.abs(x)))


FFN_TM = 512
FFN_TF = 256
FFN_SLABS = 2
STREAM_DEPTH = 4


N_FF_CHUNKS = D_FF // FFN_TF


def _row_slabs(tm):
    step = tm // FFN_SLABS
    return [slice(r * step, (r + 1) * step) for r in range(FFN_SLABS)]


def _ffn_ln_rows(x, rows, w1b, w3b, w2b, g_ref, b_ref, a_ref):
    xb = x.astype(BF16)
    for j in range(N_FF_CHUNKS):
        sl = slice(j * FFN_TF, (j + 1) * FFN_TF)
        h1 = _dot(xb, w1b[:, sl])
        h3 = _dot(xb, w3b[:, sl])
        a_ref[rows, sl] = (_silu(h1) * h3).astype(BF16)
    y = ALPHA * x + 0.5 * _dot(a_ref[rows, :], w2b[...])
    return _layer_norm(y, g_ref[...], b_ref[...])


def _stream_in(plan, sem):
    used = {}
    steps = []
    for src, stage, sid, consume in plan:
        slot = used.get(sid, 0) % STREAM_DEPTH
        used[sid] = used.get(sid, 0) + 1
        steps.append((pltpu.make_async_copy(src, stage.at[slot], sem.at[sid, slot]), stage, slot, consume))
    ahead = STREAM_DEPTH - 1
    for k in range(min(ahead, len(steps))):
        steps[k][0].start()
    for k, (copy, stage, slot, consume) in enumerate(steps):
        if k + ahead < len(steps):
            steps[k + ahead][0].start()
        copy.wait()
        consume(stage[slot])


def _store_cast(dst, rows, cols):
    def consume(v):
        dst[rows, cols] = v.astype(BF16)
    return consume


def _ffn_weight_plan(l, w1_hbm, w3_hbm, w2_hbm, w1b, w3b, w2b, col_stage, row_stage):
    plan = []
    full = slice(None)
    for j in range(N_FF_CHUNKS):
        win = pl.ds(j * FFN_TF, FFN_TF)
        sl = slice(j * FFN_TF, (j + 1) * FFN_TF)
        plan.append((w1_hbm.at[l, :, win], col_stage, 0, _store_cast(w1b, full, sl)))
        plan.append((w3_hbm.at[l, :, win], col_stage, 0, _store_cast(w3b, full, sl)))
        plan.append((w2_hbm.at[l, win, :], row_stage, 1, _store_cast(w2b, sl, full)))
    return plan


def _ffn_scratch(tm):
    return [
        pltpu.VMEM((tm, D_FF), BF16),
        pltpu.VMEM((D_MODEL, D_FF), BF16),
        pltpu.VMEM((D_MODEL, D_FF), BF16),
        pltpu.VMEM((D_FF, D_MODEL), BF16),
        pltpu.VMEM((STREAM_DEPTH, D_MODEL, FFN_TF), F32),
        pltpu.VMEM((STREAM_DEPTH, FFN_TF, D_MODEL), F32),
    ]


def _ffn_ln_kernel(l, x_ref, w1_hbm, w3_hbm, w2_hbm, g_ref, b_ref, o_ref,
                   a_ref, w1b, w3b, w2b, col_stage, row_stage, sem):
    @pl.when(pl.program_id(0) == 0)
    def _():
        _stream_in(_ffn_weight_plan(l, w1_hbm, w3_hbm, w2_hbm, w1b, w3b, w2b, col_stage, row_stage), sem)

    for rows in _row_slabs(x_ref.shape[0]):
        o_ref[rows, :] = _ffn_ln_rows(x_ref[rows, :], rows, w1b, w3b, w2b, g_ref, b_ref, a_ref)


def _layer_spec(shape, l, **kw):
    zeros = (0,) * len(shape)
    return pl.BlockSpec((None,) + tuple(shape), lambda *_: (l,) + zeros, **kw)


HBM = pl.BlockSpec(memory_space=pl.ANY)


def _ffn_ln(x, w1, w3, w2, g, b, l):
    n = x.shape[0]
    tm = min(FFN_TM, n)
    return pl.pallas_call(
        functools.partial(_ffn_ln_kernel, l),
        grid=(n // tm,),
        in_specs=[
            pl.BlockSpec((tm, D_MODEL), lambda i: (i, 0)),
            HBM, HBM, HBM,
            _layer_spec((1, D_MODEL), l),
            _layer_spec((1, D_MODEL), l),
        ],
        out_specs=pl.BlockSpec((tm, D_MODEL), lambda i: (i, 0)),
        out_shape=jax.ShapeDtypeStruct((n, D_MODEL), F32),
        scratch_shapes=_ffn_scratch(tm) + [pltpu.SemaphoreType.DMA((2, STREAM_DEPTH))],
        compiler_params=pltpu.CompilerParams(dimension_semantics=("arbitrary",), vmem_limit_bytes=VMEM_LIMIT),
        name="ffn_ln",
    )(x, w1, w3, w2, g, b)


PROJ_TM = 512
PROJ_WR = 128
D_IN = 2 * A_WIDTH + 4 * B_WIDTH + 2 * B_HEADS + 2 * C_KEY + 2 * C_WIDTH + C_RANK


def _permute_w_in(v):
    o = np.cumsum([0, A_WIDTH, A_WIDTH, 3 * B_WIDTH, B_WIDTH, B_HEADS, B_HEADS, C_KEY, C_KEY, C_WIDTH, C_WIDTH, C_RANK])
    front = v[:, 0:o[4]]
    gates_b = v[:, o[4]:o[6]]
    c_qk = v[:, o[6]:o[8]]
    c_vr = v[:, o[8]:o[10]]
    c_g = v[:, o[10]:o[11]]
    pad = jnp.zeros((v.shape[0], GATE_W - 2 * B_HEADS - C_RANK), F32)
    return jnp.concatenate([front, c_vr, c_qk, gates_b, c_g, pad], axis=-1).astype(BF16)


def _proj_sgu_kernel(tiles_per_seq, l, x_ref, w_hbm, lng_ref, lnb_ref, sw_ref, bias_ref, cw_ref, h_ref, a_ref,
                     xbuf, w_ref, w_stage, sem):
    tm = x_ref.shape[0]
    w3 = 3 * B_WIDTH
    c0 = 2 * A_WIDTH

    @pl.when(pl.program_id(0) == 0)
    def _():
        def store_rows(r):
            def consume(v):
                w_ref[r * PROJ_WR:(r + 1) * PROJ_WR, :] = _permute_w_in(v.astype(F32))
            return consume
        _stream_in([(w_hbm.at[l, pl.ds(r * PROJ_WR, PROJ_WR), :], w_stage, 0, store_rows(r))
                    for r in range(D_MODEL // PROJ_WR)], sem)

    @pl.when(pl.program_id(0) % tiles_per_seq == 0)
    def _():
        xbuf[0:TAIL, :] = jnp.zeros((TAIL, w3), F32)

    xb = x_ref[...].astype(BF16)

    def conv_silu(part):
        cols = slice(part * B_WIDTH, (part + 1) * B_WIDTH)
        acc = cw_ref[B_CONV - 1:B_CONV, cols] * xbuf[TAIL:TAIL + tm, cols]
        for i in range(B_CONV - 1):
            back = B_CONV - 1 - i
            acc = acc + cw_ref[i:i + 1, cols] * xbuf[TAIL - back:TAIL - back + tm, cols]
        xbuf[0:TAIL, cols] = xbuf[tm:tm + TAIL, cols]
        return _silu(acc)

    def project(part):
        cols = slice(part * B_WIDTH, (part + 1) * B_WIDTH)
        xbuf[TAIL:TAIL + tm, cols] = _dot(xb, w_ref[:, c0 + part * B_WIDTH:c0 + (part + 1) * B_WIDTH])

    ha = _dot(xb, w_ref[:, 0:c0])
    project(0)
    u = _gelu_tanh(ha[:, 0:A_WIDTH])
    v = _layer_norm(_gelu_tanh(ha[:, A_WIDTH:c0]), lng_ref[...], lnb_ref[...])
    project(1)
    yq = conv_silu(0)
    project(2)
    yk = conv_silu(1)
    h_ref[:, w3:] = _dot(xb, w_ref[:, c0 + w3:])
    h_ref[:, 2 * B_WIDTH:w3] = conv_silu(2)

    tril = _iota((A_CHUNK, A_CHUNK), 1) <= _iota((A_CHUNK, A_CHUNK), 0)
    head_of_lane = _iota((A_CHUNK, A_WIDTH), 1) // HEAD_DIM
    wts = [jnp.where(tril, sw_ref[h], 0.0).astype(BF16) for h in range(A_HEADS)]
    bias = bias_ref[...]
    for c in range(tm // A_CHUNK):
        rows = slice(c * A_CHUNK, (c + 1) * A_CHUNK)
        vc = v[rows]
        z = bias
        for h in range(A_HEADS):
            z = z + _dot(wts[h], jnp.where(head_of_lane == h, vc, 0.0).astype(BF16))
        a_ref[rows, :] = (u[rows] * z).astype(a_ref.dtype)

    ones2 = _group_ones(LANES, B_DK)
    for p in range(N_PAIRS):
        sl = slice(p * LANES, (p + 1) * LANES)
        yqp = yq[:, sl]
        h_ref[:, sl] = yqp * (lax.rsqrt(_group_sumsq(yqp, ones2) + 1e-6) * (B_DK ** -0.5))
        ykp = yk[:, sl]
        h_ref[:, B_WIDTH + p * LANES:B_WIDTH + (p + 1) * LANES] = ykp * lax.rsqrt(_group_sumsq(ykp, ones2) + 1e-6)


def _proj_sgu(x, w, lng, lnb, sw, bias, cw, seq, l):
    n = x.shape[0]
    tm = min(PROJ_TM, seq)
    return pl.pallas_call(
        functools.partial(_proj_sgu_kernel, seq // tm, l),
        grid=(n // tm,),
        in_specs=[
            pl.BlockSpec((tm, D_MODEL), lambda i: (i, 0)),
            HBM,
            _layer_spec((1, A_WIDTH), l),
            _layer_spec((1, A_WIDTH), l),
            _layer_spec((A_HEADS, A_CHUNK, A_CHUNK), l),
            _layer_spec((A_CHUNK, A_WIDTH), l),
            _layer_spec((B_CONV, 3 * B_WIDTH), l),
        ],
        out_specs=[pl.BlockSpec((tm, D_H), lambda i: (i, 0)), pl.BlockSpec((tm, A_WIDTH), lambda i: (i, 0))],
        out_shape=[jax.ShapeDtypeStruct((n, D_H), F32), jax.ShapeDtypeStruct((n, A_WIDTH), BF16)],
        scratch_shapes=[
            pltpu.VMEM((tm + TAIL, 3 * B_WIDTH), F32),
            pltpu.VMEM((D_MODEL, D_IN_PAD), BF16),
            pltpu.VMEM((STREAM_DEPTH, PROJ_WR, D_IN), BF16),
            pltpu.SemaphoreType.DMA((1, STREAM_DEPTH)),
        ],
        compiler_params=pltpu.CompilerParams(dimension_semantics=("arbitrary",), vmem_limit_bytes=VMEM_LIMIT),
        name="proj_sgu",
    )(x, w, lng, lnb, sw, bias, cw)


GLA_TB = 512


def _gla_kernel(q_ref, k_ref, v_ref, r_ref, g_ref, gup_ref, gb_ref, tri_ref, ng_ref, o_ref, st_ref, ob_ref):
    nb, tb = q_ref.shape[0], q_ref.shape[1]
    nc = tb // CHUNK

    @pl.when(pl.program_id(0) == 0)
    def _():
        st_ref[...] = jnp.zeros_like(st_ref)

    khead = _iota((CHUNK, C_KEY), 1) // C_DK
    vhead = _iota((CHUNK, C_WIDTH), 1) // C_DV
    causal = (_iota((CHUNK, C_WIDTH), 1) % CHUNK) <= _iota((CHUNK, C_WIDTH), 0)
    st_mask = (_iota((C_WIDTH, C_KEY), 0) // C_DV) == (_iota((C_WIDTH, C_KEY), 1) // C_DK)

    units = []
    for b in range(nb):
        logit = _dot(g_ref[b], gup_ref[...], precision=HIGHEST) + gb_ref[...]
        log_a = -_softplus(-logit) * (1.0 / C_TAU)
        g3 = _dot(tri_ref[...], _split3(log_a, 1))
        bc_all = g3[:, 0:C_KEY] + g3[:, C_KEY:2 * C_KEY] + g3[:, 2 * C_KEY:3 * C_KEY]
        for c in range(nc):
            rows = slice(c * CHUNK, (c + 1) * CHUNK)
            bc = bc_all[rows]
            btot = bc[CHUNK - 1:CHUNK, :]
            k = k_ref[b, rows, :]
            kn = k * jnp.exp(-bc)
            vb = v_ref[b, rows, :].astype(BF16)
            units.append(dict(
                b=b, rows=rows, vb=vb, gt=jnp.exp(btot),
                qd=(q_ref[b, rows, :] * ((C_DK ** -0.5) * jnp.exp(bc))).astype(BF16),
                kd=(k * jnp.exp(btot - bc)).astype(BF16),
                kstack=jnp.concatenate([jnp.where(khead == h, kn, 0.0) for h in range(C_HEADS)], axis=0).astype(BF16),
                vbd=jnp.concatenate([jnp.where(vhead == h, vb, jnp.zeros_like(vb)) for h in range(C_HEADS)], axis=0)))
    for u in units:
        u["p"] = jnp.where(causal, _dot_nt(u["qd"], u["kstack"]), 0.0).astype(BF16)
    for u in units:
        u["upd"] = jnp.where(st_mask, _dot_tn(u["vb"], u["kd"]), 0.0)
    for u in units:
        u["oi"] = _dot(u["p"], u["vbd"])

    states = [st_ref[b] for b in range(nb)]
    for u in units:
        b = u["b"]
        ob_ref[b, u["rows"], :] = u["oi"] + _dot_nt(u["qd"], states[b].astype(BF16))
        states[b] = u["gt"] * states[b] + u["upd"]
    for b in range(nb):
        st_ref[b] = states[b]

    ones = _group_ones(C_WIDTH, C_DV)
    for b in range(nb):
        o = ob_ref[b]
        ss = _group_sumsq(o, ones)
        o_ref[b] = (o * lax.rsqrt(ss * (1.0 / C_DV) + EPS) * ng_ref[...] * _silu(r_ref[b])).astype(o_ref.dtype)


def _gla(h3, gup, gb, ng, l):
    batch, seq, _ = h3.shape
    tb = min(GLA_TB, seq)
    hspec = lambda width, col: pl.BlockSpec((batch, tb, width), lambda t: (0, t, col))
    return pl.pallas_call(
        _gla_kernel,
        grid=(seq // tb,),
        in_specs=[
            hspec(C_KEY, COL_CQ), hspec(C_KEY, COL_CK), hspec(C_WIDTH, COL_CV), hspec(C_WIDTH, COL_CR),
            hspec(GATE_W, COL_G),
            _layer_spec((GATE_W, C_KEY), l),
            _layer_spec((1, C_KEY), l),
            pl.BlockSpec((tb, tb), lambda t: (0, 0)),
            _layer_spec((1, C_WIDTH), l),
        ],
        out_specs=pl.BlockSpec((batch, tb, C_WIDTH), lambda t: (0, t, 0)),
        out_shape=jax.ShapeDtypeStruct((batch, seq, C_WIDTH), BF16),
        scratch_shapes=[pltpu.VMEM((batch, C_WIDTH, C_KEY), F32), pltpu.VMEM((batch, tb, C_WIDTH), F32)],
        compiler_params=pltpu.CompilerParams(dimension_semantics=("arbitrary",), vmem_limit_bytes=VMEM_LIMIT),
        name="gla",
    )(h3, h3, h3, h3, h3, gup, gb, _chunk_triangles(tb)[0], ng)


GDN_TB = 256
GROUP_CHUNKS = 4


def _gdn_kernel(q_ref, k_ref, v_ref, z_ref, g_ref, arow_ref, dtrow_ref, ebg_ref, tri_ref, ng_ref, o_ref,
                bg_s, rows_s, qa_s, bb_s, gt_s, o_s, s_ref):
    nb, tb = q_ref.shape[0], q_ref.shape[1]
    nc = tb // CHUNK

    @pl.when(pl.program_id(0) == 0)
    def _():
        s_ref[...] = jnp.zeros_like(s_ref)

    ones2 = _group_ones(LANES, B_DK)
    lane = _iota((tb, GATE_W), 1)
    low_half = _iota((1, LANES), 1) < CHUNK
    batches = range(nb)
    nh2 = 2 * B_HEADS
    sum3 = lambda t, n, ax: (lax.slice_in_dim(t, 0, n, axis=ax) + lax.slice_in_dim(t, n, 2 * n, axis=ax)
                             + lax.slice_in_dim(t, 2 * n, 3 * n, axis=ax))
    betas = [jax.nn.sigmoid(g_ref[b]) for b in batches]
    lgs = [jnp.where((lane >= B_HEADS) & (lane < nh2),
                     -jnp.exp(arow_ref[...]) * _softplus(g_ref[b] + dtrow_ref[...]), 0.0) for b in batches]
    gams = [sum3(_dot(tri_ref[0], _split3(lg, 1)), GATE_W, 1) for lg in lgs]
    lg_t3s = [_split3(lg.T[0:nh2, :], 0) for lg in lgs]
    gam_ts = [sum3(_dot(t, tri_ref[1]), nh2, 0) for t in lg_t3s]
    gam_t_sws = [sum3(_dot(t, tri_ref[2]), nh2, 0) for t in lg_t3s]
    for b in batches:
        feat = jnp.where(lane < B_HEADS, betas[b], gams[b])
        bg_s[b] = _dot(_split2(feat), ebg_ref[...])
    for b in batches:
        for c in range(nc):
            tile = slice((c // 2) * LANES, (c // 2 + 1) * LANES)
            first, second = (gam_ts[b], gam_t_sws[b]) if c % 2 == 0 else (gam_t_sws[b], gam_ts[b])
            for p in range(N_PAIRS):
                h0 = B_HEADS + 2 * p
                rows_s[b, c, p:p + 1, :] = jnp.where(low_half, first[h0:h0 + 1, tile], second[h0 + 1:h0 + 2, tile])

    ii = _iota((CHUNK, LANES), 0)
    jj = _iota((CHUNK, LANES), 1) % CHUNK
    incl = jj <= ii
    strict = jj < ii
    eye = jnp.where(jj == ii, 1.0, 0.0)
    lo_lane = _iota((CHUNK, LANES), 1) < CHUNK

    def blockdiag(m):
        zero = jnp.zeros_like(m)
        return jnp.concatenate([jnp.where(lo_lane, m, zero), jnp.where(lo_lane, zero, m)], axis=0)

    groups_per_batch = nc // GROUP_CHUNKS

    def solve_group(g, carry):
        b = g // groups_per_batch
        c0 = (g % groups_per_batch) * GROUP_CHUNKS
        units = []
        for dc in range(GROUP_CHUNKS):
            c = c0 + dc
            rws = pl.ds(pl.multiple_of(c * CHUNK, CHUNK), CHUNK)
            grows = rows_s[b, c]
            for p in range(N_PAIRS):
                sl = slice(p * LANES, (p + 1) * LANES)
                q = q_ref[b, rws, sl]
                k = k_ref[b, rws, sl]
                v = v_ref[b, rws, sl]
                bet = bg_s[b, rws, sl]
                gi = bg_s[b, rws, B_WIDTH + p * LANES:B_WIDTH + (p + 1) * LANES]
                dec = jnp.exp(jnp.minimum(gi - grows[p:p + 1, :], 0.0))
                eg = jnp.exp(gi)
                glast = gi[CHUNK - 1:CHUNK, :]
                k16 = k.astype(BF16)
                qk = _dot_nt(jnp.concatenate([q.astype(BF16), k16], axis=0), blockdiag(k16))
                nk = jnp.where(strict, qk[CHUNK:2 * CHUNK] * (-bet) * dec, 0.0)
                gt_s[b, c, p:p + 1, :] = jnp.exp(glast)
                units.append(dict(
                    c=c, p=p, rws=rws, sl=sl,
                    pm=jnp.where(incl, qk[0:CHUNK] * dec, 0.0).astype(BF16),
                    tm=eye + nk, nk=nk.astype(BF16),
                    rhs=jnp.concatenate([blockdiag((v * bet).astype(BF16)),
                                         blockdiag((k * (bet * eg)).astype(BF16))], axis=1),
                    qd=q * eg, kd=(k * jnp.exp(glast - gi)).astype(BF16)))
        for u in units:
            u["nk"] = _dot(u["nk"], blockdiag(u["nk"])).astype(BF16)
        for _ in range(4):
            for u in units:
                both = _dot(u["nk"], jnp.concatenate([blockdiag(u["nk"]), blockdiag(u["tm"].astype(BF16))], axis=1))
                u["nk"] = both[:, 0:LANES].astype(BF16)
                u["tm"] = u["tm"] + both[:, LANES:2 * LANES]
        for u in units:
            u["tm"] = u["tm"] + _dot(u["nk"], blockdiag(u["tm"].astype(BF16)))
        for u in units:
            u["sol"] = _dot(u["tm"].astype(BF16), u["rhs"]).astype(BF16)
        for u in units:
            ab = _dot_tn(u["kd"], u["sol"])
            bb_s[b, u["c"], u["p"]] = jnp.where(lo_lane, ab[0:CHUNK, 0:LANES], ab[CHUNK:2 * CHUNK, 0:LANES])
            qa_s[b, u["c"], u["p"], CHUNK:2 * CHUNK, :] = (
                -jnp.where(lo_lane, ab[0:CHUNK, LANES:2 * LANES], ab[CHUNK:2 * CHUNK, LANES:2 * LANES])).astype(BF16)
        for u in units:
            sol = u["sol"]
            po = _dot(u["pm"], jnp.concatenate([blockdiag(sol[:, 0:LANES]), blockdiag(sol[:, LANES:2 * LANES])], axis=1))
            o_s[b, u["rws"], u["sl"]] = po[:, 0:LANES]
            qa_s[b, u["c"], u["p"], 0:CHUNK, :] = (u["qd"] - po[:, LANES:2 * LANES]).astype(BF16)
        return carry

    lax.fori_loop(0, nb * groups_per_batch, solve_group, 0)

    def scan_chunk(c, carry):
        rws = pl.ds(pl.multiple_of(c * CHUNK, CHUNK), CHUNK)
        chains = [(b, p) for b in range(nb) for p in range(N_PAIRS)]
        states = [s_ref[b, p] for b, p in chains]
        prods = [_dot(qa_s[b, c, p], blockdiag(s.astype(BF16))) for (b, p), s in zip(chains, states)]
        for (b, p), s, r in zip(chains, states, prods):
            sl = slice(p * LANES, (p + 1) * LANES)
            s_ref[b, p] = gt_s[b, c, p:p + 1, :] * s + r[CHUNK:2 * CHUNK] + bb_s[b, c, p]
            o_s[b, rws, sl] = o_s[b, rws, sl] + r[0:CHUNK]
        return carry

    lax.fori_loop(0, nc, scan_chunk, 0)

    slabs = [(b, slice(p * LANES, (p + 1) * LANES)) for b in range(nb) for p in range(N_PAIRS)]
    sums = [_group_sumsq(o_s[b, :, sl], ones2) for b, sl in slabs]
    for (b, sl), ss in zip(slabs, sums):
        o_ref[b, :, sl] = (o_s[b, :, sl] * lax.rsqrt(ss * (1.0 / 64.0) + EPS) * ng_ref[...]
                           * _silu(z_ref[b, :, sl])).astype(o_ref.dtype)


def _chunk_triangles(tb):
    r = np.arange(tb)[:, None]
    c = np.arange(tb)[None, :]
    same = (r // CHUNK) == (c // CHUNK)
    c_sw = np.where((c % LANES) < CHUNK, c + CHUNK, c - CHUNK)
    tril = same & (c <= r)
    triu = same & (r <= c)
    triu_sw = ((r // CHUNK) == (c_sw // CHUNK)) & (r <= c_sw)
    return jnp.asarray(np.stack([tril, triu, triu_sw]).astype(np.float32), BF16)


def _gdn(h3, arow, dtrow, ebg, ng, l):
    batch, seq, _ = h3.shape
    tb = min(GDN_TB, seq)
    nc = tb // CHUNK
    hspec = lambda width, col: pl.BlockSpec((batch, tb, width), lambda t: (0, t, col))
    return pl.pallas_call(
        _gdn_kernel,
        grid=(seq // tb,),
        in_specs=[
            hspec(B_WIDTH, COL_BQ), hspec(B_WIDTH, COL_BK), hspec(B_WIDTH, COL_BV), hspec(B_WIDTH, COL_BZ),
            hspec(GATE_W, COL_G),
            _layer_spec((1, GATE_W), l),
            _layer_spec((1, GATE_W), l),
            pl.BlockSpec((2 * GATE_W, 2 * B_WIDTH), lambda t: (0, 0)),
            pl.BlockSpec((3, tb, tb), lambda t: (0, 0, 0)),
            _layer_spec((1, LANES), l),
        ],
        out_specs=pl.BlockSpec((batch, tb, B_WIDTH), lambda t: (0, t, 0)),
        out_shape=jax.ShapeDtypeStruct((batch, seq, B_WIDTH), BF16),
        scratch_shapes=[
            pltpu.VMEM((batch, tb, 2 * B_WIDTH), F32),
            pltpu.VMEM((batch, nc, 8, LANES), F32),
            pltpu.VMEM((batch, nc, N_PAIRS, 2 * CHUNK, LANES), BF16),
            pltpu.VMEM((batch, nc, N_PAIRS, CHUNK, LANES), F32),
            pltpu.VMEM((batch, nc, 8, LANES), F32),
            pltpu.VMEM((batch, tb, B_WIDTH), F32),
            pltpu.VMEM((batch, N_PAIRS, CHUNK, LANES), F32),
        ],
        compiler_params=pltpu.CompilerParams(dimension_semantics=("arbitrary",), vmem_limit_bytes=VMEM_LIMIT),
        name="gdn",
    )(h3, h3, h3, h3, h3, arow, dtrow, ebg, _chunk_triangles(tb), ng)


def _out_ffn_kernel(l, x_ref, a_ref, b_ref, c_ref, wo_hbm, g2_ref, b2_ref, w1_hbm, w3_hbm, w2_hbm, g3_ref, b3_ref,
                    o_ref, act_ref, w1_ref, w3_ref, w2_ref, col_stage, row_stage, wo_ref, sem):
    @pl.when(pl.program_id(0) == 0)
    def _():
        plan = _ffn_weight_plan(l, w1_hbm, w3_hbm, w2_hbm, w1_ref, w3_ref, w2_ref, col_stage, row_stage)
        plan += [(wo_hbm.at[l, pl.ds(j * FFN_TF, FFN_TF), :], row_stage, 1,
                  _store_cast(wo_ref, slice(j * FFN_TF, (j + 1) * FFN_TF), slice(None)))
                 for j in range(D_MODEL // FFN_TF)]
        _stream_in(plan, sem)

    slabs = _row_slabs(x_ref.shape[0])
    mixes = []
    for rows in slabs:
        mix = _dot(a_ref[rows, :], wo_ref[0:A_WIDTH, :])
        mix = mix + _dot(b_ref[rows, :], wo_ref[A_WIDTH:A_WIDTH + B_WIDTH, :])
        mixes.append(mix + _dot(c_ref[rows, :], wo_ref[A_WIDTH + B_WIDTH:, :]))
    for rows, mix in zip(slabs, mixes):
        x2 = _layer_norm(ALPHA * x_ref[rows, :] + mix, g2_ref[...], b2_ref[...])
        o_ref[rows, :] = _ffn_ln_rows(x2, rows, w1_ref, w3_ref, w2_ref, g3_ref, b3_ref, act_ref)


def _out_ffn(x, a, b, c, wo, g2, b2, w1, w3, w2, g3, b3, l):
    n = x.shape[0]
    tm = min(FFN_TM, n)
    rows = lambda width: pl.BlockSpec((tm, width), lambda i: (i, 0))
    return pl.pallas_call(
        functools.partial(_out_ffn_kernel, l),
        grid=(n // tm,),
        in_specs=[
            rows(D_MODEL), rows(A_WIDTH), rows(B_WIDTH), rows(C_WIDTH),
            HBM,
            _layer_spec((1, D_MODEL), l),
            _layer_spec((1, D_MODEL), l),
            HBM, HBM, HBM,
            _layer_spec((1, D_MODEL), l),
            _layer_spec((1, D_MODEL), l),
        ],
        out_specs=rows(D_MODEL),
        out_shape=jax.ShapeDtypeStruct((n, D_MODEL), F32),
        scratch_shapes=_ffn_scratch(tm) + [pltpu.VMEM((D_MODEL, D_MODEL), BF16),
                                           pltpu.SemaphoreType.DMA((2, STREAM_DEPTH))],
        compiler_params=pltpu.CompilerParams(dimension_semantics=("arbitrary",), vmem_limit_bytes=VMEM_LIMIT),
        name="out_ffn",
    )(x, a, b, c, wo, g2, b2, w1, w3, w2, g3, b3)


def _bcast_selector():
    e = np.zeros((2 * GATE_W, 2 * B_WIDTH), np.float32)
    for j in range(B_WIDTH):
        hd = j // 64
        for base in (0, GATE_W):
            e[base + hd, j] = 1.0
            e[base + B_HEADS + hd, B_WIDTH + j] = 1.0
    return jnp.asarray(e, BF16)


def _gate_row(vec, offset):
    l, n = vec.shape
    return jnp.zeros((l, 1, GATE_W), F32).at[:, 0, offset:offset + n].set(vec)


def kernel(x, ffn1_w1, ffn1_w3, ffn1_w2, ln1_g, ln1_b, w_in, sgu_ln_g, sgu_ln_b, sgu_w, sgu_b, gdn_conv_w, gdn_a_log, gdn_dt_bias, gdn_norm_g, gla_gate_up, gla_gate_b, gla_norm_g, w_out, ln2_g, ln2_b, ffn2_w1, ffn2_w3, ffn2_w2, ln3_g, ln3_b):
    batch, seq, d = x.shape
    depth = w_in.shape[0]
    row = lambda p: p[:, None, :]
    f1 = (ffn1_w1, ffn1_w3, ffn1_w2)
    f2 = (ffn2_w1, ffn2_w3, ffn2_w2)
    w_in_b = w_in.astype(BF16)
    sgu_bias = jnp.repeat(jnp.swapaxes(sgu_b, 1, 2), HEAD_DIM, axis=2)
    arow = _gate_row(gdn_a_log, B_HEADS)
    dtrow = _gate_row(gdn_dt_bias, B_HEADS)
    ebg = _bcast_selector()
    gdn_ng = jnp.tile(gdn_norm_g, (1, 2))[:, None, :]
    gla_ng = jnp.tile(gla_norm_g, (1, C_HEADS))[:, None, :]
    gup = jnp.zeros((depth, GATE_W, C_KEY), F32).at[:, 2 * B_HEADS:2 * B_HEADS + C_RANK, :].set(gla_gate_up)

    xf = x.reshape(batch * seq, d)
    for l in range(depth):
        xf = _ffn_ln(xf, *f1, row(ln1_g), row(ln1_b), l)
        h, out_a = _proj_sgu(xf, w_in_b, row(sgu_ln_g), row(sgu_ln_b), sgu_w, sgu_bias, gdn_conv_w, seq, l)
        h3 = h.reshape(batch, seq, D_H)
        out_b = _gdn(h3, arow, dtrow, ebg, gdn_ng, l).reshape(batch * seq, B_WIDTH)
        out_c = _gla(h3, gup, row(gla_gate_b), gla_ng, l).reshape(batch * seq, C_WIDTH)
        xf = _out_ffn(xf, out_a, out_b, out_c, w_out, row(ln2_g), row(ln2_b), *f2, row(ln3_g), row(ln3_b), l)
    return xf.reshape(batch, seq, d)
```

```python
import functools

import numpy as np
import jax
import jax.numpy as jnp
from jax import lax
from jax.experimental import pallas as pl
from jax.experimental.pallas import tpu as pltpu

F32 = jnp.float32
BF16 = jnp.bfloat16

D_MODEL = 1024
DEPTH = 4
HEAD_DIM = 64
A_HEADS = 4
A_WIDTH = A_HEADS * HEAD_DIM
A_CHUNK = 128
B_HEADS = 8
B_DK = 64
B_WIDTH = B_HEADS * 64
B_CONV = 4
C_HEADS = 4
C_DK = 32
C_DV = 64
C_KEY = C_HEADS * C_DK
C_WIDTH = C_HEADS * C_DV
C_RANK = 16
C_TAU = 16.0
D_FF = 2816
ALPHA = (2.0 * DEPTH) ** 0.25
EPS = 1e-5

LANES = 128
CHUNK = 64
GATE_W = LANES
N_PAIRS = B_HEADS // 2
TAIL = 8
D_IN_PAD = 2 * A_WIDTH + 4 * B_WIDTH + 2 * C_WIDTH + 2 * C_KEY + GATE_W
VMEM_LIMIT = 56 * 1024 * 1024

D_H = D_IN_PAD - 2 * A_WIDTH

COL_BQ, COL_BK, COL_BV, COL_BZ = 0, 1, 2, 3
COL_CV, COL_CR = 8, 9
COL_CQ, COL_CK, COL_G = 20, 21, 22


def _dot(a, b, **kw):
    return jnp.dot(a, b, preferred_element_type=F32, **kw)


def _dot_nt(a, b, **kw):
    return lax.dot_general(a, b, (((1,), (1,)), ((), ())), preferred_element_type=F32, **kw)


def _dot_tn(a, b, **kw):
    return lax.dot_general(a, b, (((0,), (0,)), ((), ())), preferred_element_type=F32, **kw)


def _split2(x):
    hi = x.astype(BF16)
    lo = (x - hi.astype(F32)).astype(BF16)
    return jnp.concatenate([hi, lo], axis=-1)


def _split3(x, axis):
    hi = x.astype(BF16)
    r1 = x - hi.astype(F32)
    mid = r1.astype(BF16)
    lo = (r1 - mid.astype(F32)).astype(BF16)
    return jnp.concatenate([hi, mid, lo], axis=axis)


def _iota(shape, dim):
    return lax.broadcasted_iota(jnp.int32, shape, dim)


def _group_ones(width, group):
    r = _iota((width, width), 0)
    c = _iota((width, width), 1)
    return jnp.where((r // group) == (c // group), 1.0, 0.0).astype(BF16)


def _group_sumsq(x, ones):
    return _dot((x * x).astype(BF16), ones)


def _layer_norm(y, g, b):
    mu = jnp.mean(y, axis=-1, keepdims=True)
    yc = y - mu
    var = jnp.mean(yc * yc, axis=-1, keepdims=True)
    return yc * lax.rsqrt(var + EPS) * g + b


def _silu(x):
    return x * jax.nn.sigmoid(x)


def _gelu_tanh(x):
    return 0.5 * x * (1.0 + jnp.tanh(0.7978845608028654 * (x + 0.044715 * (x * x * x))))


def _softplus(x):
    return jnp.maximum(x, 0.0) + jnp.log(1.0 + jnp.exp(-jnp.abs(x)))


FFN_TM = 512
FFN_TF = 256
FFN_SLABS = 2
STREAM_DEPTH = 4


N_FF_CHUNKS = D_FF // FFN_TF


def _row_slabs(tm):
    step = tm // FFN_SLABS
    return [slice(r * step, (r + 1) * step) for r in range(FFN_SLABS)]


def _ffn_ln_rows(x, rows, w1b, w3b, w2b, g_ref, b_ref, a_ref):
    xb = x.astype(BF16)
    for j in range(N_FF_CHUNKS):
        sl = slice(j * FFN_TF, (j + 1) * FFN_TF)
        h1 = _dot(xb, w1b[:, sl])
        h3 = _dot(xb, w3b[:, sl])
        a_ref[rows, sl] = (_silu(h1) * h3).astype(BF16)
    y = ALPHA * x + 0.5 * _dot(a_ref[rows, :], w2b[...])
    return _layer_norm(y, g_ref[...], b_ref[...])


def _stream_in(plan, sem):
    used = {}
    steps = []
    for src, stage, sid, consume in plan:
        slot = used.get(sid, 0) % STREAM_DEPTH
        used[sid] = used.get(sid, 0) + 1
        steps.append((pltpu.make_async_copy(src, stage.at[slot], sem.at[sid, slot]), stage, slot, consume))
    ahead = STREAM_DEPTH - 1
    for k in range(min(ahead, len(steps))):
        steps[k][0].start()
    for k, (copy, stage, slot, consume) in enumerate(steps):
        if k + ahead < len(steps):
            steps[k + ahead][0].start()
        copy.wait()
        consume(stage[slot])


def _store_cast(dst, rows, cols):
    def consume(v):
        dst[rows, cols] = v.astype(BF16)
    return consume


def _ffn_weight_plan(l, w1_hbm, w3_hbm, w2_hbm, w1b, w3b, w2b, col_stage, row_stage):
    plan = []
    full = slice(None)
    for j in range(N_FF_CHUNKS):
        win = pl.ds(j * FFN_TF, FFN_TF)
        sl = slice(j * FFN_TF, (j + 1) * FFN_TF)
        plan.append((w1_hbm.at[l, :, win], col_stage, 0, _store_cast(w1b, full, sl)))
        plan.append((w3_hbm.at[l, :, win], col_stage, 0, _store_cast(w3b, full, sl)))
        plan.append((w2_hbm.at[l, win, :], row_stage, 1, _store_cast(w2b, sl, full)))
    return plan


def _ffn_scratch(tm):
    return [
        pltpu.VMEM((tm, D_FF), BF16),
        pltpu.VMEM((D_MODEL, D_FF), BF16),
        pltpu.VMEM((D_MODEL, D_FF), BF16),
        pltpu.VMEM((D_FF, D_MODEL), BF16),
        pltpu.VMEM((STREAM_DEPTH, D_MODEL, FFN_TF), F32),
        pltpu.VMEM((STREAM_DEPTH, FFN_TF, D_MODEL), F32),
    ]


def _ffn_ln_kernel(l, x_ref, w1_hbm, w3_hbm, w2_hbm, g_ref, b_ref, o_ref,
                   a_ref, w1b, w3b, w2b, col_stage, row_stage, sem):
    @pl.when(pl.program_id(0) == 0)
    def _():
        _stream_in(_ffn_weight_plan(l, w1_hbm, w3_hbm, w2_hbm, w1b, w3b, w2b, col_stage, row_stage), sem)

    for rows in _row_slabs(x_ref.shape[0]):
        o_ref[rows, :] = _ffn_ln_rows(x_ref[rows, :], rows, w1b, w3b, w2b, g_ref, b_ref, a_ref)


def _layer_spec(shape, l, **kw):
    zeros = (0,) * len(shape)
    return pl.BlockSpec((None,) + tuple(shape), lambda *_: (l,) + zeros, **kw)


HBM = pl.BlockSpec(memory_space=pl.ANY)


def _ffn_ln(x, w1, w3, w2, g, b, l):
    n = x.shape[0]
    tm = min(FFN_TM, n)
    return pl.pallas_call(
        functools.partial(_ffn_ln_kernel, l),
        grid=(n // tm,),
        in_specs=[
            pl.BlockSpec((tm, D_MODEL), lambda i: (i, 0)),
            HBM, HBM, HBM,
            _layer_spec((1, D_MODEL), l),
            _layer_spec((1, D_MODEL), l),
        ],
        out_specs=pl.BlockSpec((tm, D_MODEL), lambda i: (i, 0)),
        out_shape=jax.ShapeDtypeStruct((n, D_MODEL), F32),
        scratch_shapes=_ffn_scratch(tm) + [pltpu.SemaphoreType.DMA((2, STREAM_DEPTH))],
        compiler_params=pltpu.CompilerParams(dimension_semantics=("arbitrary",), vmem_limit_bytes=VMEM_LIMIT),
        name="ffn_ln",
    )(x, w1, w3, w2, g, b)


PROJ_TM = 512
PROJ_WR = 128
D_IN = 2 * A_WIDTH + 4 * B_WIDTH + 2 * B_HEADS + 2 * C_KEY + 2 * C_WIDTH + C_RANK


def _permute_w_in(v):
    o = np.cumsum([0, A_WIDTH, A_WIDTH, 3 * B_WIDTH, B_WIDTH, B_HEADS, B_HEADS, C_KEY, C_KEY, C_WIDTH, C_WIDTH, C_RANK])
    front = v[:, 0:o[4]]
    gates_b = v[:, o[4]:o[6]]
    c_qk = v[:, o[6]:o[8]]
    c_vr = v[:, o[8]:o[10]]
    c_g = v[:, o[10]:o[11]]
    pad = jnp.zeros((v.shape[0], GATE_W - 2 * B_HEADS - C_RANK), F32)
    return jnp.concatenate([front, c_vr, c_qk, gates_b, c_g, pad], axis=-1).astype(BF16)


def _proj_sgu_kernel(tiles_per_seq, l, x_ref, w_hbm, lng_ref, lnb_ref, sw_ref, bias_ref, cw_ref, h_ref, a_ref,
                     xbuf, w_ref, w_stage, sem):
    tm = x_ref.shape[0]
    w3 = 3 * B_WIDTH
    c0 = 2 * A_WIDTH

    @pl.when(pl.program_id(0) == 0)
    def _():
        def store_rows(r):
            def consume(v):
                w_ref[r * PROJ_WR:(r + 1) * PROJ_WR, :] = _permute_w_in(v.astype(F32))
            return consume
        _stream_in([(w_hbm.at[l, pl.ds(r * PROJ_WR, PROJ_WR), :], w_stage, 0, store_rows(r))
                    for r in range(D_MODEL // PROJ_WR)], sem)

    @pl.when(pl.program_id(0) % tiles_per_seq == 0)
    def _():
        xbuf[0:TAIL, :] = jnp.zeros((TAIL, w3), F32)

    xb = x_ref[...].astype(BF16)

    def conv_silu(part):
        cols = slice(part * B_WIDTH, (part + 1) * B_WIDTH)
        acc = cw_ref[B_CONV - 1:B_CONV, cols] * xbuf[TAIL:TAIL + tm, cols]
        for i in range(B_CONV - 1):
            back = B_CONV - 1 - i
            acc = acc + cw_ref[i:i + 1, cols] * xbuf[TAIL - back:TAIL - back + tm, cols]
        xbuf[0:TAIL, cols] = xbuf[tm:tm + TAIL, cols]
        return _silu(acc)

    def project(part):
        cols = slice(part * B_WIDTH, (part + 1) * B_WIDTH)
        xbuf[TAIL:TAIL + tm, cols] = _dot(xb, w_ref[:, c0 + part * B_WIDTH:c0 + (part + 1) * B_WIDTH])

    ones2 = _group_ones(LANES, B_DK)

    def l2_normalise(y, part, scale):
        for p in range(N_PAIRS):
            yp = y[:, p * LANES:(p + 1) * LANES]
            h_ref[:, part * B_WIDTH + p * LANES:part * B_WIDTH + (p + 1) * LANES] = (
                yp * (lax.rsqrt(_group_sumsq(yp, ones2) + 1e-6) * scale))

    ha = _dot(xb, w_ref[:, 0:c0])
    project(0)
    u = _gelu_tanh(ha[:, 0:A_WIDTH])
    v = _layer_norm(_gelu_tanh(ha[:, A_WIDTH:c0]), lng_ref[...], lnb_ref[...])
    project(1)

    tril = _iota((A_CHUNK, A_CHUNK), 1) <= _iota((A_CHUNK, A_CHUNK), 0)
    head_of_lane = _iota((A_CHUNK, A_WIDTH), 1) // HEAD_DIM
    wts = [jnp.where(tril, sw_ref[h], 0.0).astype(BF16) for h in range(A_HEADS)]
    bias = bias_ref[...]
    for c in range(tm // A_CHUNK):
        rows = slice(c * A_CHUNK, (c + 1) * A_CHUNK)
        vc = v[rows]
        z = bias
        for h in range(A_HEADS):
            z = z + _dot(wts[h], jnp.where(head_of_lane == h, vc, 0.0).astype(BF16))
        a_ref[rows, :] = (u[rows] * z).astype(a_ref.dtype)

    yq = conv_silu(0)
    project(2)
    yk = conv_silu(1)
    l2_normalise(yq, 0, B_DK ** -0.5)
    h_ref[:, w3:] = _dot(xb, w_ref[:, c0 + w3:])
    h_ref[:, 2 * B_WIDTH:w3] = conv_silu(2)
    l2_normalise(yk, 1, 1.0)


def _proj_sgu(x, w, lng, lnb, sw, bias, cw, seq, l):
    n = x.shape[0]
    tm = min(PROJ_TM, seq)
    return pl.pallas_call(
        functools.partial(_proj_sgu_kernel, seq // tm, l),
        grid=(n // tm,),
        in_specs=[
            pl.BlockSpec((tm, D_MODEL), lambda i: (i, 0)),
            HBM,
            _layer_spec((1, A_WIDTH), l),
            _layer_spec((1, A_WIDTH), l),
            _layer_spec((A_HEADS, A_CHUNK, A_CHUNK), l),
            _layer_spec((A_CHUNK, A_WIDTH), l),
            _layer_spec((B_CONV, 3 * B_WIDTH), l),
        ],
        out_specs=[pl.BlockSpec((tm, D_H), lambda i: (i, 0)), pl.BlockSpec((tm, A_WIDTH), lambda i: (i, 0))],
        out_shape=[jax.ShapeDtypeStruct((n, D_H), F32), jax.ShapeDtypeStruct((n, A_WIDTH), BF16)],
        scratch_shapes=[
            pltpu.VMEM((tm + TAIL, 3 * B_WIDTH), F32),
            pltpu.VMEM((D_MODEL, D_IN_PAD), BF16),
            pltpu.VMEM((STREAM_DEPTH, PROJ_WR, D_IN), BF16),
            pltpu.SemaphoreType.DMA((1, STREAM_DEPTH)),
        ],
        compiler_params=pltpu.CompilerParams(dimension_semantics=("arbitrary",), vmem_limit_bytes=VMEM_LIMIT),
        name="proj_sgu",
    )(x, w, lng, lnb, sw, bias, cw)


GLA_TB = 512
GLA_TRI = 256


def _gla_kernel(q_ref, k_ref, v_ref, r_ref, g_ref, gup_ref, gb_ref, tri_ref, ng_ref, o_ref, st_ref, ob_ref):
    nb, tb = q_ref.shape[0], q_ref.shape[1]
    nc = tb // CHUNK

    @pl.when(pl.program_id(0) == 0)
    def _():
        st_ref[...] = jnp.zeros_like(st_ref)

    khead = _iota((CHUNK, C_KEY), 1) // C_DK
    vhead = _iota((CHUNK, C_WIDTH), 1) // C_DV
    causal = (_iota((CHUNK, C_WIDTH), 1) % CHUNK) <= _iota((CHUNK, C_WIDTH), 0)
    st_mask = (_iota((C_WIDTH, C_KEY), 0) // C_DV) == (_iota((C_WIDTH, C_KEY), 1) // C_DK)

    units = []
    gup = gup_ref[...]
    u_hi = gup.astype(BF16)
    u_lo = (gup - u_hi.astype(F32)).astype(BF16)
    gup3 = jnp.concatenate([u_hi, u_hi, u_lo], axis=0)
    tri_rows = tri_ref.shape[0]
    for b in range(nb):
        g = g_ref[b]
        g_hi = g.astype(BF16)
        g_lo = (g - g_hi.astype(F32)).astype(BF16)
        logit = _dot(jnp.concatenate([g_hi, g_lo, g_hi], axis=1), gup3) + gb_ref[...]
        log_a = -_softplus(-logit) * (1.0 / C_TAU)
        bc_parts = []
        for r in range(tb // tri_rows):
            g3 = _dot(tri_ref[...], _split3(log_a[r * tri_rows:(r + 1) * tri_rows], 1))
            bc_parts.append(g3[:, 0:C_KEY] + g3[:, C_KEY:2 * C_KEY] + g3[:, 2 * C_KEY:3 * C_KEY])
        for c in range(nc):
            rows = slice(c * CHUNK, (c + 1) * CHUNK)
            part, off = divmod(c * CHUNK, tri_rows)
            bc = bc_parts[part][off:off + CHUNK]
            btot = bc[CHUNK - 1:CHUNK, :]
            k = k_ref[b, rows, :]
            kn = k * jnp.exp(-bc)
            vb = v_ref[b, rows, :].astype(BF16)
            units.append(dict(
                b=b, rows=rows, vb=vb, gt=jnp.exp(btot),
                qd=(q_ref[b, rows, :] * ((C_DK ** -0.5) * jnp.exp(bc))).astype(BF16),
                kd=(k * jnp.exp(btot - bc)).astype(BF16),
                kstack=jnp.concatenate([jnp.where(khead == h, kn, 0.0) for h in range(C_HEADS)], axis=0).astype(BF16),
                vbd=jnp.concatenate([jnp.where(vhead == h, vb, jnp.zeros_like(vb)) for h in range(C_HEADS)], axis=0)))
    for u in units:
        u["p"] = jnp.where(causal, _dot_nt(u["qd"], u["kstack"]), 0.0).astype(BF16)
    for u in units:
        u["upd"] = jnp.where(st_mask, _dot_tn(u["vb"], u["kd"]), 0.0)
    for u in units:
        u["oi"] = _dot(u["p"], u["vbd"])

    states = [st_ref[b] for b in range(nb)]
    for u in units:
        b = u["b"]
        ob_ref[b, u["rows"], :] = u["oi"] + _dot_nt(u["qd"], states[b].astype(BF16))
        states[b] = u["gt"] * states[b] + u["upd"]
    for b in range(nb):
        st_ref[b] = states[b]

    ones = _group_ones(C_WIDTH, C_DV)
    for b in range(nb):
        o = ob_ref[b]
        ss = _group_sumsq(o, ones)
        o_ref[b] = (o * lax.rsqrt(ss * (1.0 / C_DV) + EPS) * ng_ref[...] * _silu(r_ref[b])).astype(o_ref.dtype)


def _gla(h3, gup, gb, ng, l):
    batch, seq, _ = h3.shape
    tb = min(GLA_TB, seq)
    hspec = lambda width, col: pl.BlockSpec((batch, tb, width), lambda t: (0, t, col))
    return pl.pallas_call(
        _gla_kernel,
        grid=(seq // tb,),
        in_specs=[
            hspec(C_KEY, COL_CQ), hspec(C_KEY, COL_CK), hspec(C_WIDTH, COL_CV), hspec(C_WIDTH, COL_CR),
            hspec(GATE_W, COL_G),
            _layer_spec((GATE_W, C_KEY), l),
            _layer_spec((1, C_KEY), l),
            pl.BlockSpec((GLA_TRI, GLA_TRI), lambda t: (0, 0)),
            _layer_spec((1, C_WIDTH), l),
        ],
        out_specs=pl.BlockSpec((batch, tb, C_WIDTH), lambda t: (0, t, 0)),
        out_shape=jax.ShapeDtypeStruct((batch, seq, C_WIDTH), BF16),
        scratch_shapes=[pltpu.VMEM((batch, C_WIDTH, C_KEY), F32), pltpu.VMEM((batch, tb, C_WIDTH), F32)],
        compiler_params=pltpu.CompilerParams(dimension_semantics=("arbitrary",), vmem_limit_bytes=VMEM_LIMIT),
        name="gla",
    )(h3, h3, h3, h3, h3, gup, gb, _chunk_triangles(GLA_TRI)[0], ng)


GDN_TB = 256
GROUP_CHUNKS = 4


def _gdn_kernel(q_ref, k_ref, v_ref, z_ref, g_ref, arow_ref, dtrow_ref, ebg_ref, tri_ref, ng_ref, o_ref,
                bg_s, rows_s, qa_s, bb_s, gt_s, o_s, s_ref):
    nb, tb = q_ref.shape[0], q_ref.shape[1]
    nc = tb // CHUNK

    @pl.when(pl.program_id(0) == 0)
    def _():
        s_ref[...] = jnp.zeros_like(s_ref)

    ones2 = _group_ones(LANES, B_DK)
    lane = _iota((tb, GATE_W), 1)
    low_half = _iota((1, LANES), 1) < CHUNK
    batches = range(nb)
    nh2 = 2 * B_HEADS
    sum3 = lambda t, n, ax: (lax.slice_in_dim(t, 0, n, axis=ax) + lax.slice_in_dim(t, n, 2 * n, axis=ax)
                             + lax.slice_in_dim(t, 2 * n, 3 * n, axis=ax))
    betas = [jax.nn.sigmoid(g_ref[b]) for b in batches]
    lgs = [jnp.where((lane >= B_HEADS) & (lane < nh2),
                     -jnp.exp(arow_ref[...]) * _softplus(g_ref[b] + dtrow_ref[...]), 0.0) for b in batches]
    gams = [sum3(_dot(tri_ref[0], _split3(lg, 1)), GATE_W, 1) for lg in lgs]
    lg_t3s = [_split3(lg.T[0:nh2, :], 0) for lg in lgs]
    gam_ts = [sum3(_dot(t, tri_ref[1]), nh2, 0) for t in lg_t3s]
    gam_t_sws = [sum3(_dot(t, tri_ref[2]), nh2, 0) for t in lg_t3s]
    for b in batches:
        feat = jnp.where(lane < B_HEADS, betas[b], gams[b])
        bg_s[b] = _dot(_split2(feat), ebg_ref[...])
    for b in batches:
        for c in range(nc):
            tile = slice((c // 2) * LANES, (c // 2 + 1) * LANES)
            first, second = (gam_ts[b], gam_t_sws[b]) if c % 2 == 0 else (gam_t_sws[b], gam_ts[b])
            for p in range(N_PAIRS):
                h0 = B_HEADS + 2 * p
                rows_s[b, c, p:p + 1, :] = jnp.where(low_half, first[h0:h0 + 1, tile], second[h0 + 1:h0 + 2, tile])

    ii = _iota((CHUNK, LANES), 0)
    jj = _iota((CHUNK, LANES), 1) % CHUNK
    incl = jj <= ii
    strict = jj < ii
    eye = jnp.where(jj == ii, 1.0, 0.0)
    lo_lane = _iota((CHUNK, LANES), 1) < CHUNK

    def blockdiag(m):
        zero = jnp.zeros_like(m)
        return jnp.concatenate([jnp.where(lo_lane, m, zero), jnp.where(lo_lane, zero, m)], axis=0)

    groups_per_batch = nc // GROUP_CHUNKS

    def solve_group(g, carry):
        b = g // groups_per_batch
        c0 = (g % groups_per_batch) * GROUP_CHUNKS
        units = []
        for dc in range(GROUP_CHUNKS):
            c = c0 + dc
            rws = pl.ds(pl.multiple_of(c * CHUNK, CHUNK), CHUNK)
            grows = rows_s[b, c]
            for p in range(N_PAIRS):
                sl = slice(p * LANES, (p + 1) * LANES)
                q = q_ref[b, rws, sl]
                k = k_ref[b, rws, sl]
                v = v_ref[b, rws, sl]
                bet = bg_s[b, rws, sl]
                gi = bg_s[b, rws, B_WIDTH + p * LANES:B_WIDTH + (p + 1) * LANES]
                dec = jnp.exp(jnp.minimum(gi - grows[p:p + 1, :], 0.0))
                eg = jnp.exp(gi)
                glast = gi[CHUNK - 1:CHUNK, :]
                k16 = k.astype(BF16)
                qk = _dot_nt(jnp.concatenate([q.astype(BF16), k16], axis=0), blockdiag(k16))
                nk = jnp.where(strict, qk[CHUNK:2 * CHUNK] * (-bet) * dec, 0.0)
                gt_s[b, c, p:p + 1, :] = jnp.exp(glast)
                units.append(dict(
                    c=c, p=p, rws=rws, sl=sl,
                    pm=jnp.where(incl, qk[0:CHUNK] * dec, 0.0).astype(BF16),
                    tm=eye + nk, nk=nk.astype(BF16),
                    rhs=jnp.concatenate([blockdiag((v * bet).astype(BF16)),
                                         blockdiag((k * (bet * eg)).astype(BF16))], axis=1),
                    qd=q * eg, kd=(k * jnp.exp(glast - gi)).astype(BF16)))
        for u in units:
            u["nk"] = _dot(u["nk"], blockdiag(u["nk"])).astype(BF16)
        for _ in range(4):
            for u in units:
                both = _dot(u["nk"], jnp.concatenate([blockdiag(u["nk"]), blockdiag(u["tm"].astype(BF16))], axis=1))
                u["nk"] = both[:, 0:LANES].astype(BF16)
                u["tm"] = u["tm"] + both[:, LANES:2 * LANES]
        for u in units:
            u["tm"] = u["tm"] + _dot(u["nk"], blockdiag(u["tm"].astype(BF16)))
        for u in units:
            u["sol"] = _dot(u["tm"].astype(BF16), u["rhs"]).astype(BF16)
        for u in units:
            ab = _dot_tn(u["kd"], u["sol"])
            bb_s[b, u["c"], u["p"]] = jnp.where(lo_lane, ab[0:CHUNK, 0:LANES], ab[CHUNK:2 * CHUNK, 0:LANES])
            qa_s[b, u["c"], u["p"], CHUNK:2 * CHUNK, :] = (
                -jnp.where(lo_lane, ab[0:CHUNK, LANES:2 * LANES], ab[CHUNK:2 * CHUNK, LANES:2 * LANES])).astype(BF16)
        for u in units:
            sol = u["sol"]
            po = _dot(u["pm"], jnp.concatenate([blockdiag(sol[:, 0:LANES]), blockdiag(sol[:, LANES:2 * LANES])], axis=1))
            o_s[b, u["rws"], u["sl"]] = po[:, 0:LANES]
            qa_s[b, u["c"], u["p"], 0:CHUNK, :] = (u["qd"] - po[:, LANES:2 * LANES]).astype(BF16)
        return carry

    lax.fori_loop(0, nb * groups_per_batch, solve_group, 0)

    def scan_chunk(c, carry):
        rws = pl.ds(pl.multiple_of(c * CHUNK, CHUNK), CHUNK)
        chains = [(b, p) for b in range(nb) for p in range(N_PAIRS)]
        states = [s_ref[b, p] for b, p in chains]
        prods = [_dot(qa_s[b, c, p], blockdiag(s.astype(BF16))) for (b, p), s in zip(chains, states)]
        for (b, p), s, r in zip(chains, states, prods):
            sl = slice(p * LANES, (p + 1) * LANES)
            s_ref[b, p] = gt_s[b, c, p:p + 1, :] * s + r[CHUNK:2 * CHUNK] + bb_s[b, c, p]
            o_s[b, rws, sl] = o_s[b, rws, sl] + r[0:CHUNK]
        return carry

    lax.fori_loop(0, nc, scan_chunk, 0)

    slabs = [(b, slice(p * LANES, (p + 1) * LANES)) for b in range(nb) for p in range(N_PAIRS)]
    sums = [_group_sumsq(o_s[b, :, sl], ones2) for b, sl in slabs]
    for (b, sl), ss in zip(slabs, sums):
        o_ref[b, :, sl] = (o_s[b, :, sl] * lax.rsqrt(ss * (1.0 / 64.0) + EPS) * ng_ref[...]
                           * _silu(z_ref[b, :, sl])).astype(o_ref.dtype)


def _chunk_triangles(tb):
    r = np.arange(tb)[:, None]
    c = np.arange(tb)[None, :]
    same = (r // CHUNK) == (c // CHUNK)
    c_sw = np.where((c % LANES) < CHUNK, c + CHUNK, c - CHUNK)
    tril = same & (c <= r)
    triu = same & (r <= c)
    triu_sw = ((r // CHUNK) == (c_sw // CHUNK)) & (r <= c_sw)
    return jnp.asarray(np.stack([tril, triu, triu_sw]).astype(np.float32), BF16)


def _gdn(h3, arow, dtrow, ebg, ng, l):
    batch, seq, _ = h3.shape
    tb = min(GDN_TB, seq)
    nc = tb // CHUNK
    hspec = lambda width, col: pl.BlockSpec((batch, tb, width), lambda t: (0, t, col))
    return pl.pallas_call(
        _gdn_kernel,
        grid=(seq // tb,),
        in_specs=[
            hspec(B_WIDTH, COL_BQ), hspec(B_WIDTH, COL_BK), hspec(B_WIDTH, COL_BV), hspec(B_WIDTH, COL_BZ),
            hspec(GATE_W, COL_G),
            _layer_spec((1, GATE_W), l),
            _layer_spec((1, GATE_W), l),
            pl.BlockSpec((2 * GATE_W, 2 * B_WIDTH), lambda t: (0, 0)),
            pl.BlockSpec((3, tb, tb), lambda t: (0, 0, 0)),
            _layer_spec((1, LANES), l),
        ],
        out_specs=pl.BlockSpec((batch, tb, B_WIDTH), lambda t: (0, t, 0)),
        out_shape=jax.ShapeDtypeStruct((batch, seq, B_WIDTH), BF16),
        scratch_shapes=[
            pltpu.VMEM((batch, tb, 2 * B_WIDTH), F32),
            pltpu.VMEM((batch, nc, 8, LANES), F32),
            pltpu.VMEM((batch, nc, N_PAIRS, 2 * CHUNK, LANES), BF16),
            pltpu.VMEM((batch, nc, N_PAIRS, CHUNK, LANES), F32),
            pltpu.VMEM((batch, nc, 8, LANES), F32),
            pltpu.VMEM((batch, tb, B_WIDTH), F32),
            pltpu.VMEM((batch, N_PAIRS, CHUNK, LANES), F32),
        ],
        compiler_params=pltpu.CompilerParams(dimension_semantics=("arbitrary",), vmem_limit_bytes=VMEM_LIMIT),
        name="gdn",
    )(h3, h3, h3, h3, h3, arow, dtrow, ebg, _chunk_triangles(tb), ng)


def _out_ffn_kernel(l, x_ref, a_ref, b_ref, c_ref, wo_hbm, g2_ref, b2_ref, w1_hbm, w3_hbm, w2_hbm, g3_ref, b3_ref,
                    o_ref, act_ref, w1_ref, w3_ref, w2_ref, col_stage, row_stage, wo_ref, sem):
    @pl.when(pl.program_id(0) == 0)
    def _():
        plan = _ffn_weight_plan(l, w1_hbm, w3_hbm, w2_hbm, w1_ref, w3_ref, w2_ref, col_stage, row_stage)
        plan += [(wo_hbm.at[l, pl.ds(j * FFN_TF, FFN_TF), :], row_stage, 1,
                  _store_cast(wo_ref, slice(j * FFN_TF, (j + 1) * FFN_TF), slice(None)))
                 for j in range(D_MODEL // FFN_TF)]
        _stream_in(plan, sem)

    slabs = _row_slabs(x_ref.shape[0])
    mixes = []
    for rows in slabs:
        mix = _dot(a_ref[rows, :], wo_ref[0:A_WIDTH, :])
        mix = mix + _dot(b_ref[rows, :], wo_ref[A_WIDTH:A_WIDTH + B_WIDTH, :])
        mixes.append(mix + _dot(c_ref[rows, :], wo_ref[A_WIDTH + B_WIDTH:, :]))
    for rows, mix in zip(slabs, mixes):
        x2 = _layer_norm(ALPHA * x_ref[rows, :] + mix, g2_ref[...], b2_ref[...])
        o_ref[rows, :] = _ffn_ln_rows(x2, rows, w1_ref, w3_ref, w2_ref, g3_ref, b3_ref, act_ref)


def _out_ffn(x, a, b, c, wo, g2, b2, w1, w3, w2, g3, b3, l):
    n = x.shape[0]
    tm = min(FFN_TM, n)
    rows = lambda width: pl.BlockSpec((tm, width), lambda i: (i, 0))
    return pl.pallas_call(
        functools.partial(_out_ffn_kernel, l),
        grid=(n // tm,),
        in_specs=[
            rows(D_MODEL), rows(A_WIDTH), rows(B_WIDTH), rows(C_WIDTH),
            HBM,
            _layer_spec((1, D_MODEL), l),
            _layer_spec((1, D_MODEL), l),
            HBM, HBM, HBM,
            _layer_spec((1, D_MODEL), l),
            _layer_spec((1, D_MODEL), l),
        ],
        out_specs=rows(D_MODEL),
        out_shape=jax.ShapeDtypeStruct((n, D_MODEL), F32),
        scratch_shapes=_ffn_scratch(tm) + [pltpu.VMEM((D_MODEL, D_MODEL), BF16),
                                           pltpu.SemaphoreType.DMA((2, STREAM_DEPTH))],
        compiler_params=pltpu.CompilerParams(dimension_semantics=("arbitrary",), vmem_limit_bytes=VMEM_LIMIT),
        name="out_ffn",
    )(x, a, b, c, wo, g2, b2, w1, w3, w2, g3, b3)


def _bcast_selector():
    e = np.zeros((2 * GATE_W, 2 * B_WIDTH), np.float32)
    for j in range(B_WIDTH):
        hd = j // 64
        for base in (0, GATE_W):
            e[base + hd, j] = 1.0
            e[base + B_HEADS + hd, B_WIDTH + j] = 1.0
    return jnp.asarray(e, BF16)


def _gate_row(vec, offset):
    l, n = vec.shape
    return jnp.zeros((l, 1, GATE_W), F32).at[:, 0, offset:offset + n].set(vec)


def kernel(x, ffn1_w1, ffn1_w3, ffn1_w2, ln1_g, ln1_b, w_in, sgu_ln_g, sgu_ln_b, sgu_w, sgu_b, gdn_conv_w, gdn_a_log, gdn_dt_bias, gdn_norm_g, gla_gate_up, gla_gate_b, gla_norm_g, w_out, ln2_g, ln2_b, ffn2_w1, ffn2_w3, ffn2_w2, ln3_g, ln3_b):
    batch, seq, d = x.shape
    depth = w_in.shape[0]
    row = lambda p: p[:, None, :]
    f1 = (ffn1_w1, ffn1_w3, ffn1_w2)
    f2 = (ffn2_w1, ffn2_w3, ffn2_w2)
    w_in_b = w_in.astype(BF16)
    sgu_bias = jnp.repeat(jnp.swapaxes(sgu_b, 1, 2), HEAD_DIM, axis=2)
    arow = _gate_row(gdn_a_log, B_HEADS)
    dtrow = _gate_row(gdn_dt_bias, B_HEADS)
    ebg = _bcast_selector()
    gdn_ng = jnp.tile(gdn_norm_g, (1, 2))[:, None, :]
    gla_ng = jnp.tile(gla_norm_g, (1, C_HEADS))[:, None, :]
    gup = jnp.zeros((depth, GATE_W, C_KEY), F32).at[:, 2 * B_HEADS:2 * B_HEADS + C_RANK, :].set(gla_gate_up)

    xf = x.reshape(batch * seq, d)
    for l in range(depth):
        xf = _ffn_ln(xf, *f1, row(ln1_g), row(ln1_b), l)
        h, out_a = _proj_sgu(xf, w_in_b, row(sgu_ln_g), row(sgu_ln_b), sgu_w, sgu_bias, gdn_conv_w, seq, l)
        h3 = h.reshape(batch, seq, D_H)
        out_b = _gdn(h3, arow, dtrow, ebg, gdn_ng, l).reshape(batch * seq, B_WIDTH)
        out_c = _gla(h3, gup, row(gla_gate_b), gla_ng, l).reshape(batch * seq, C_WIDTH)
        xf = _out_ffn(xf, out_a, out_b, out_c, w_out, row(ln2_g), row(ln2_b), *f2, row(ln3_g), row(ln3_b), l)
    return xf.reshape(batch, seq, d)
```

```python
import functools

import numpy as np
import jax
import jax.numpy as jnp
from jax import lax
from jax.experimental import pallas as pl
from jax.experimental.pallas import tpu as pltpu

F32 = jnp.float32
BF16 = jnp.bfloat16

D_MODEL = 1024
DEPTH = 4
HEAD_DIM = 64
A_HEADS = 4
A_WIDTH = A_HEADS * HEAD_DIM
A_CHUNK = 128
B_HEADS = 8
B_DK = 64
B_WIDTH = B_HEADS * 64
B_CONV = 4
C_HEADS = 4
C_DK = 32
C_DV = 64
C_KEY = C_HEADS * C_DK
C_WIDTH = C_HEADS * C_DV
C_RANK = 16
C_TAU = 16.0
D_FF = 2816
ALPHA = (2.0 * DEPTH) ** 0.25
EPS = 1e-5

LANES = 128
CHUNK = 64
GATE_W = LANES
N_PAIRS = B_HEADS // 2
TAIL = 8
D_IN_PAD = 2 * A_WIDTH + 4 * B_WIDTH + 2 * C_WIDTH + 2 * C_KEY + GATE_W
VMEM_LIMIT = 56 * 1024 * 1024

D_H = D_IN_PAD - 2 * A_WIDTH

COL_BQ, COL_BK, COL_BV, COL_BZ = 0, 1, 2, 3
COL_CV, COL_CR = 8, 9
COL_CQ, COL_CK, COL_G = 20, 21, 22


def _dot(a, b, **kw):
    return jnp.dot(a, b, preferred_element_type=F32, **kw)


def _dot_nt(a, b, **kw):
    return lax.dot_general(a, b, (((1,), (1,)), ((), ())), preferred_element_type=F32, **kw)


def _dot_tn(a, b, **kw):
    return lax.dot_general(a, b, (((0,), (0,)), ((), ())), preferred_element_type=F32, **kw)


def _split2(x):
    hi = x.astype(BF16)
    lo = (x - hi.astype(F32)).astype(BF16)
    return jnp.concatenate([hi, lo], axis=-1)


def _split3(x, axis):
    hi = x.astype(BF16)
    r1 = x - hi.astype(F32)
    mid = r1.astype(BF16)
    lo = (r1 - mid.astype(F32)).astype(BF16)
    return jnp.concatenate([hi, mid, lo], axis=axis)


def _iota(shape, dim):
    return lax.broadcasted_iota(jnp.int32, shape, dim)


def _group_ones(width, group):
    r = _iota((width, width), 0)
    c = _iota((width, width), 1)
    return jnp.where((r // group) == (c // group), 1.0, 0.0).astype(BF16)


def _group_sumsq(x, ones):
    return _dot((x * x).astype(BF16), ones)


def _layer_norm(y, g, b):
    mu = jnp.mean(y, axis=-1, keepdims=True)
    yc = y - mu
    var = jnp.mean(yc * yc, axis=-1, keepdims=True)
    return yc * lax.rsqrt(var + EPS) * g + b


def _silu(x):
    return x * jax.nn.sigmoid(x)


def _gelu_tanh(x):
    return 0.5 * x * (1.0 + jnp.tanh(0.7978845608028654 * (x + 0.044715 * (x * x * x))))


def _softplus(x):
    return jnp.maximum(x, 0.0) + jnp.log(1.0 + jnp.exp(-jnp.abs(x)))


FFN_TM = 1024
OUT_FFN_TM = 512
FFN_TF = 256
FFN_SLAB_ROWS = 256
STREAM_DEPTH = 4


N_FF_CHUNKS = D_FF // FFN_TF


def _row_slabs(tm):
    return [slice(r, r + FFN_SLAB_ROWS) for r in range(0, tm, FFN_SLAB_ROWS)]


def _ffn_ln_rows(x, rows, w1b, w3b, w2b, g_ref, b_ref, a_ref):
    xb = x.astype(BF16)
    for j in range(N_FF_CHUNKS):
        sl = slice(j * FFN_TF, (j + 1) * FFN_TF)
        h1 = _dot(xb, w1b[:, sl])
        h3 = _dot(xb, w3b[:, sl])
        a_ref[rows, sl] = (_silu(h1) * h3).astype(BF16)
    y = ALPHA * x + 0.5 * _dot(a_ref[rows, :], w2b[...])
    return _layer_norm(y, g_ref[...], b_ref[...])


def _stream_in(plan, sem):
    used = {}
    steps = []
    for src, stage, sid, consume in plan:
        slot = used.get(sid, 0) % STREAM_DEPTH
        used[sid] = used.get(sid, 0) + 1
        steps.append((pltpu.make_async_copy(src, stage.at[slot], sem.at[sid, slot]), stage, slot, consume))
    ahead = STREAM_DEPTH - 1
    for k in range(min(ahead, len(steps))):
        steps[k][0].start()
    for k, (copy, stage, slot, consume) in enumerate(steps):
        if k + ahead < len(steps):
            steps[k + ahead][0].start()
        copy.wait()
        consume(stage[slot])


def _store_cast(dst, rows, cols):
    def consume(v):
        dst[rows, cols] = v.astype(BF16)
    return consume


def _ffn_weight_plan(l, w1_hbm, w3_hbm, w2_hbm, w1b, w3b, w2b, col_stage, row_stage):
    plan = []
    full = slice(None)
    for j in range(N_FF_CHUNKS):
        win = pl.ds(j * FFN_TF, FFN_TF)
        sl = slice(j * FFN_TF, (j + 1) * FFN_TF)
        plan.append((w1_hbm.at[l, :, win], col_stage, 0, _store_cast(w1b, full, sl)))
        plan.append((w3_hbm.at[l, :, win], col_stage, 0, _store_cast(w3b, full, sl)))
        plan.append((w2_hbm.at[l, win, :], row_stage, 1, _store_cast(w2b, sl, full)))
    return plan


def _ffn_scratch(tm):
    return [
        pltpu.VMEM((tm, D_FF), BF16),
        pltpu.VMEM((D_MODEL, D_FF), BF16),
        pltpu.VMEM((D_MODEL, D_FF), BF16),
        pltpu.VMEM((D_FF, D_MODEL), BF16),
        pltpu.VMEM((STREAM_DEPTH, D_MODEL, FFN_TF), F32),
        pltpu.VMEM((STREAM_DEPTH, FFN_TF, D_MODEL), F32),
    ]


def _ffn_ln_kernel(l, x_ref, w1_hbm, w3_hbm, w2_hbm, g_ref, b_ref, o_ref,
                   a_ref, w1b, w3b, w2b, col_stage, row_stage, sem):
    @pl.when(pl.program_id(0) == 0)
    def _():
        _stream_in(_ffn_weight_plan(l, w1_hbm, w3_hbm, w2_hbm, w1b, w3b, w2b, col_stage, row_stage), sem)

    for rows in _row_slabs(x_ref.shape[0]):
        o_ref[rows, :] = _ffn_ln_rows(x_ref[rows, :], rows, w1b, w3b, w2b, g_ref, b_ref, a_ref)


def _layer_spec(shape, l, **kw):
    zeros = (0,) * len(shape)
    return pl.BlockSpec((None,) + tuple(shape), lambda *_: (l,) + zeros, **kw)


HBM = pl.BlockSpec(memory_space=pl.ANY)


def _ffn_ln(x, w1, w3, w2, g, b, l):
    n = x.shape[0]
    tm = min(FFN_TM, n)
    return pl.pallas_call(
        functools.partial(_ffn_ln_kernel, l),
        grid=(n // tm,),
        in_specs=[
            pl.BlockSpec((tm, D_MODEL), lambda i: (i, 0)),
            HBM, HBM, HBM,
            _layer_spec((1, D_MODEL), l),
            _layer_spec((1, D_MODEL), l),
        ],
        out_specs=pl.BlockSpec((tm, D_MODEL), lambda i: (i, 0)),
        out_shape=jax.ShapeDtypeStruct((n, D_MODEL), F32),
        scratch_shapes=_ffn_scratch(tm) + [pltpu.SemaphoreType.DMA((2, STREAM_DEPTH))],
        compiler_params=pltpu.CompilerParams(dimension_semantics=("arbitrary",), vmem_limit_bytes=VMEM_LIMIT),
        name="ffn_ln",
    )(x, w1, w3, w2, g, b)


PROJ_TM = 512
PROJ_WR = 128
D_IN = 2 * A_WIDTH + 4 * B_WIDTH + 2 * B_HEADS + 2 * C_KEY + 2 * C_WIDTH + C_RANK


def _permute_w_in(v):
    o = np.cumsum([0, A_WIDTH, A_WIDTH, 3 * B_WIDTH, B_WIDTH, B_HEADS, B_HEADS, C_KEY, C_KEY, C_WIDTH, C_WIDTH, C_RANK])
    front = v[:, 0:o[4]]
    gates_b = v[:, o[4]:o[6]]
    c_qk = v[:, o[6]:o[8]]
    c_vr = v[:, o[8]:o[10]]
    c_g = v[:, o[10]:o[11]]
    pad = jnp.zeros((v.shape[0], GATE_W - 2 * B_HEADS - C_RANK), F32)
    return jnp.concatenate([front, c_vr, c_qk, gates_b, c_g, pad], axis=-1).astype(BF16)


def _proj_sgu_kernel(tiles_per_seq, l, x_ref, w_hbm, lng_ref, lnb_ref, sw_ref, bias_ref, cw_ref, h_ref, a_ref,
                     xbuf, w_ref, w_stage, sem):
    tm = x_ref.shape[0]
    w3 = 3 * B_WIDTH
    c0 = 2 * A_WIDTH

    @pl.when(pl.program_id(0) == 0)
    def _():
        def store_rows(r):
            def consume(v):
                w_ref[r * PROJ_WR:(r + 1) * PROJ_WR, :] = _permute_w_in(v.astype(F32))
            return consume
        _stream_in([(w_hbm.at[l, pl.ds(r * PROJ_WR, PROJ_WR), :], w_stage, 0, store_rows(r))
                    for r in range(D_MODEL // PROJ_WR)], sem)

    @pl.when(pl.program_id(0) % tiles_per_seq == 0)
    def _():
        xbuf[0:TAIL, :] = jnp.zeros((TAIL, w3), F32)

    xb = x_ref[...].astype(BF16)

    def conv_silu(part):
        cols = slice(part * B_WIDTH, (part + 1) * B_WIDTH)
        acc = cw_ref[B_CONV - 1:B_CONV, cols] * xbuf[TAIL:TAIL + tm, cols]
        for i in range(B_CONV - 1):
            back = B_CONV - 1 - i
            acc = acc + cw_ref[i:i + 1, cols] * xbuf[TAIL - back:TAIL - back + tm, cols]
        xbuf[0:TAIL, cols] = xbuf[tm:tm + TAIL, cols]
        return _silu(acc)

    def project(part):
        cols = slice(part * B_WIDTH, (part + 1) * B_WIDTH)
        xbuf[TAIL:TAIL + tm, cols] = _dot(xb, w_ref[:, c0 + part * B_WIDTH:c0 + (part + 1) * B_WIDTH])

    ones2 = _group_ones(LANES, B_DK)

    def l2_normalise(y, part, scale):
        for p in range(N_PAIRS):
            yp = y[:, p * LANES:(p + 1) * LANES]
            h_ref[:, part * B_WIDTH + p * LANES:part * B_WIDTH + (p + 1) * LANES] = (
                yp * (lax.rsqrt(_group_sumsq(yp, ones2) + 1e-6) * scale))

    ha = _dot(xb, w_ref[:, 0:c0])
    project(0)
    u = _gelu_tanh(ha[:, 0:A_WIDTH])
    v = _layer_norm(_gelu_tanh(ha[:, A_WIDTH:c0]), lng_ref[...], lnb_ref[...])
    project(1)

    tril = _iota((A_CHUNK, A_CHUNK), 1) <= _iota((A_CHUNK, A_CHUNK), 0)
    head_of_lane = _iota((A_CHUNK, A_WIDTH), 1) // HEAD_DIM
    wts = [jnp.where(tril, sw_ref[h], 0.0).astype(BF16) for h in range(A_HEADS)]
    bias = bias_ref[...]
    for c in range(tm // A_CHUNK):
        rows = slice(c * A_CHUNK, (c + 1) * A_CHUNK)
        vc = v[rows]
        z = bias
        for h in range(A_HEADS):
            z = z + _dot(wts[h], jnp.where(head_of_lane == h, vc, 0.0).astype(BF16))
        a_ref[rows, :] = (u[rows] * z).astype(a_ref.dtype)

    yq = conv_silu(0)
    project(2)
    yk = conv_silu(1)
    l2_normalise(yq, 0, B_DK ** -0.5)
    h_ref[:, w3:] = _dot(xb, w_ref[:, c0 + w3:])
    h_ref[:, 2 * B_WIDTH:w3] = conv_silu(2)
    l2_normalise(yk, 1, 1.0)


def _proj_sgu(x, w, lng, lnb, sw, bias, cw, seq, l):
    n = x.shape[0]
    tm = min(PROJ_TM, seq)
    return pl.pallas_call(
        functools.partial(_proj_sgu_kernel, seq // tm, l),
        grid=(n // tm,),
        in_specs=[
            pl.BlockSpec((tm, D_MODEL), lambda i: (i, 0)),
            HBM,
            _layer_spec((1, A_WIDTH), l),
            _layer_spec((1, A_WIDTH), l),
            _layer_spec((A_HEADS, A_CHUNK, A_CHUNK), l),
            _layer_spec((A_CHUNK, A_WIDTH), l),
            _layer_spec((B_CONV, 3 * B_WIDTH), l),
        ],
        out_specs=[pl.BlockSpec((tm, D_H), lambda i: (i, 0)), pl.BlockSpec((tm, A_WIDTH), lambda i: (i, 0))],
        out_shape=[jax.ShapeDtypeStruct((n, D_H), F32), jax.ShapeDtypeStruct((n, A_WIDTH), BF16)],
        scratch_shapes=[
            pltpu.VMEM((tm + TAIL, 3 * B_WIDTH), F32),
            pltpu.VMEM((D_MODEL, D_IN_PAD), BF16),
            pltpu.VMEM((STREAM_DEPTH, PROJ_WR, D_IN), BF16),
            pltpu.SemaphoreType.DMA((1, STREAM_DEPTH)),
        ],
        compiler_params=pltpu.CompilerParams(dimension_semantics=("arbitrary",), vmem_limit_bytes=VMEM_LIMIT),
        name="proj_sgu",
    )(x, w, lng, lnb, sw, bias, cw)


GLA_TB = 512
GLA_TRI = 256


def _gla_kernel(q_ref, k_ref, v_ref, r_ref, g_ref, gup_ref, gb_ref, tri_ref, ng_ref, o_ref, st_ref, ob_ref):
    nb, tb = q_ref.shape[0], q_ref.shape[1]
    nc = tb // CHUNK

    @pl.when(pl.program_id(0) == 0)
    def _():
        st_ref[...] = jnp.zeros_like(st_ref)

    khead = _iota((CHUNK, C_KEY), 1) // C_DK
    vhead = _iota((CHUNK, C_WIDTH), 1) // C_DV
    causal = (_iota((CHUNK, C_WIDTH), 1) % CHUNK) <= _iota((CHUNK, C_WIDTH), 0)
    st_mask = (_iota((C_WIDTH, C_KEY), 0) // C_DV) == (_iota((C_WIDTH, C_KEY), 1) // C_DK)

    units = []
    gup = gup_ref[...]
    u_hi = gup.astype(BF16)
    u_lo = (gup - u_hi.astype(F32)).astype(BF16)
    gup3 = jnp.concatenate([u_hi, u_hi, u_lo], axis=0)
    tri_rows = tri_ref.shape[0]
    for b in range(nb):
        g = g_ref[b]
        g_hi = g.astype(BF16)
        g_lo = (g - g_hi.astype(F32)).astype(BF16)
        logit = _dot(jnp.concatenate([g_hi, g_lo, g_hi], axis=1), gup3) + gb_ref[...]
        log_a = -_softplus(-logit) * (1.0 / C_TAU)
        bc_parts = []
        for r in range(tb // tri_rows):
            g3 = _dot(tri_ref[...], _split3(log_a[r * tri_rows:(r + 1) * tri_rows], 1))
            bc_parts.append(g3[:, 0:C_KEY] + g3[:, C_KEY:2 * C_KEY] + g3[:, 2 * C_KEY:3 * C_KEY])
        for c in range(nc):
            rows = slice(c * CHUNK, (c + 1) * CHUNK)
            part, off = divmod(c * CHUNK, tri_rows)
            bc = bc_parts[part][off:off + CHUNK]
            btot = bc[CHUNK - 1:CHUNK, :]
            k = k_ref[b, rows, :]
            kn = k * jnp.exp(-bc)
            vb = v_ref[b, rows, :].astype(BF16)
            units.append(dict(
                b=b, rows=rows, vb=vb, gt=jnp.exp(btot),
                qd=(q_ref[b, rows, :] * ((C_DK ** -0.5) * jnp.exp(bc))).astype(BF16),
                kd=(k * jnp.exp(btot - bc)).astype(BF16),
                kstack=jnp.concatenate([jnp.where(khead == h, kn, 0.0) for h in range(C_HEADS)], axis=0).astype(BF16),
                vbd=jnp.concatenate([jnp.where(vhead == h, vb, jnp.zeros_like(vb)) for h in range(C_HEADS)], axis=0)))
    for u in units:
        u["p"] = jnp.where(causal, _dot_nt(u["qd"], u["kstack"]), 0.0).astype(BF16)
    for u in units:
        u["upd"] = jnp.where(st_mask, _dot_tn(u["vb"], u["kd"]), 0.0)
    for u in units:
        u["oi"] = _dot(u["p"], u["vbd"])

    states = [st_ref[b] for b in range(nb)]
    for u in units:
        b = u["b"]
        ob_ref[b, u["rows"], :] = u["oi"] + _dot_nt(u["qd"], states[b].astype(BF16))
        states[b] = u["gt"] * states[b] + u["upd"]
    for b in range(nb):
        st_ref[b] = states[b]

    ones = _group_ones(C_WIDTH, C_DV)
    for b in range(nb):
        o = ob_ref[b]
        ss = _group_sumsq(o, ones)
        o_ref[b] = (o * lax.rsqrt(ss * (1.0 / C_DV) + EPS) * ng_ref[...] * _silu(r_ref[b])).astype(o_ref.dtype)


def _gla(h3, gup, gb, ng, l):
    batch, seq, _ = h3.shape
    tb = min(GLA_TB, seq)
    hspec = lambda width, col: pl.BlockSpec((batch, tb, width), lambda t: (0, t, col))
    return pl.pallas_call(
        _gla_kernel,
        grid=(seq // tb,),
        in_specs=[
            hspec(C_KEY, COL_CQ), hspec(C_KEY, COL_CK), hspec(C_WIDTH, COL_CV), hspec(C_WIDTH, COL_CR),
            hspec(GATE_W, COL_G),
            _layer_spec((GATE_W, C_KEY), l),
            _layer_spec((1, C_KEY), l),
            pl.BlockSpec((GLA_TRI, GLA_TRI), lambda t: (0, 0)),
            _layer_spec((1, C_WIDTH), l),
        ],
        out_specs=pl.BlockSpec((batch, tb, C_WIDTH), lambda t: (0, t, 0)),
        out_shape=jax.ShapeDtypeStruct((batch, seq, C_WIDTH), BF16),
        scratch_shapes=[pltpu.VMEM((batch, C_WIDTH, C_KEY), F32), pltpu.VMEM((batch, tb, C_WIDTH), F32)],
        compiler_params=pltpu.CompilerParams(dimension_semantics=("arbitrary",), vmem_limit_bytes=VMEM_LIMIT),
        name="gla",
    )(h3, h3, h3, h3, h3, gup, gb, _chunk_triangles(GLA_TRI)[0], ng)


GDN_TB = 256
GROUP_CHUNKS = 4


def _gdn_kernel(q_ref, k_ref, v_ref, z_ref, g_ref, arow_ref, dtrow_ref, ebg_ref, tri_ref, ng_ref, o_ref,
                bg_s, rows_s, qa_s, bb_s, gt_s, o_s, s_ref):
    nb, tb = q_ref.shape[0], q_ref.shape[1]
    nc = tb // CHUNK

    @pl.when(pl.program_id(0) == 0)
    def _():
        s_ref[...] = jnp.zeros_like(s_ref)

    ones2 = _group_ones(LANES, B_DK)
    lane = _iota((tb, GATE_W), 1)
    low_half = _iota((1, LANES), 1) < CHUNK
    batches = range(nb)
    nh2 = 2 * B_HEADS
    sum3 = lambda t, n, ax: (lax.slice_in_dim(t, 0, n, axis=ax) + lax.slice_in_dim(t, n, 2 * n, axis=ax)
                             + lax.slice_in_dim(t, 2 * n, 3 * n, axis=ax))
    betas = [jax.nn.sigmoid(g_ref[b]) for b in batches]
    lgs = [jnp.where((lane >= B_HEADS) & (lane < nh2),
                     -jnp.exp(arow_ref[...]) * _softplus(g_ref[b] + dtrow_ref[...]), 0.0) for b in batches]
    gams = [sum3(_dot(tri_ref[0], _split3(lg, 1)), GATE_W, 1) for lg in lgs]
    lg_t3s = [_split3(lg.T[0:nh2, :], 0) for lg in lgs]
    gam_ts = [sum3(_dot(t, tri_ref[1]), nh2, 0) for t in lg_t3s]
    gam_t_sws = [sum3(_dot(t, tri_ref[2]), nh2, 0) for t in lg_t3s]
    for b in batches:
        feat = jnp.where(lane < B_HEADS, betas[b], gams[b])
        bg_s[b] = _dot(_split2(feat), ebg_ref[...])
    for b in batches:
        for c in range(nc):
            tile = slice((c // 2) * LANES, (c // 2 + 1) * LANES)
            first, second = (gam_ts[b], gam_t_sws[b]) if c % 2 == 0 else (gam_t_sws[b], gam_ts[b])
            for p in range(N_PAIRS):
                h0 = B_HEADS + 2 * p
                rows_s[b, c, p:p + 1, :] = jnp.where(low_half, first[h0:h0 + 1, tile], second[h0 + 1:h0 + 2, tile])

    ii = _iota((CHUNK, LANES), 0)
    jj = _iota((CHUNK, LANES), 1) % CHUNK
    incl = jj <= ii
    strict = jj < ii
    eye = jnp.where(jj == ii, 1.0, 0.0)
    lo_lane = _iota((CHUNK, LANES), 1) < CHUNK

    def blockdiag(m):
        zero = jnp.zeros_like(m)
        return jnp.concatenate([jnp.where(lo_lane, m, zero), jnp.where(lo_lane, zero, m)], axis=0)

    groups_per_batch = nc // GROUP_CHUNKS

    def solve_group(g, carry):
        b = g // groups_per_batch
        c0 = (g % groups_per_batch) * GROUP_CHUNKS
        units = []
        for dc in range(GROUP_CHUNKS):
            c = c0 + dc
            rws = pl.ds(pl.multiple_of(c * CHUNK, CHUNK), CHUNK)
            grows = rows_s[b, c]
            for p in range(N_PAIRS):
                sl = slice(p * LANES, (p + 1) * LANES)
                q = q_ref[b, rws, sl]
                k = k_ref[b, rws, sl]
                v = v_ref[b, rws, sl]
                bet = bg_s[b, rws, sl]
                gi = bg_s[b, rws, B_WIDTH + p * LANES:B_WIDTH + (p + 1) * LANES]
                dec = jnp.exp(jnp.minimum(gi - grows[p:p + 1, :], 0.0))
                eg = jnp.exp(gi)
                glast = gi[CHUNK - 1:CHUNK, :]
                k16 = k.astype(BF16)
                qk = _dot_nt(jnp.concatenate([q.astype(BF16), k16], axis=0), blockdiag(k16))
                nk = jnp.where(strict, qk[CHUNK:2 * CHUNK] * (-bet) * dec, 0.0)
                gt_s[b, c, p:p + 1, :] = jnp.exp(glast)
                units.append(dict(
                    c=c, p=p, rws=rws, sl=sl,
                    pm=jnp.where(incl, qk[0:CHUNK] * dec, 0.0).astype(BF16),
                    tm=eye + nk, nk=nk.astype(BF16),
                    rhs=jnp.concatenate([blockdiag((v * bet).astype(BF16)),
                                         blockdiag((k * (bet * eg)).astype(BF16))], axis=1),
                    qd=q * eg, kd=(k * jnp.exp(glast - gi)).astype(BF16)))
        for u in units:
            u["nk"] = _dot(u["nk"], blockdiag(u["nk"])).astype(BF16)
        for _ in range(4):
            for u in units:
                both = _dot(u["nk"], jnp.concatenate([blockdiag(u["nk"]), blockdiag(u["tm"].astype(BF16))], axis=1))
                u["nk"] = both[:, 0:LANES].astype(BF16)
                u["tm"] = u["tm"] + both[:, LANES:2 * LANES]
        for u in units:
            u["tm"] = u["tm"] + _dot(u["nk"], blockdiag(u["tm"].astype(BF16)))
        for u in units:
            u["sol"] = _dot(u["tm"].astype(BF16), u["rhs"]).astype(BF16)
        for u in units:
            ab = _dot_tn(u["kd"], u["sol"])
            bb_s[b, u["c"], u["p"]] = jnp.where(lo_lane, ab[0:CHUNK, 0:LANES], ab[CHUNK:2 * CHUNK, 0:LANES])
            qa_s[b, u["c"], u["p"], CHUNK:2 * CHUNK, :] = (
                -jnp.where(lo_lane, ab[0:CHUNK, LANES:2 * LANES], ab[CHUNK:2 * CHUNK, LANES:2 * LANES])).astype(BF16)
        for u in units:
            sol = u["sol"]
            po = _dot(u["pm"], jnp.concatenate([blockdiag(sol[:, 0:LANES]), blockdiag(sol[:, LANES:2 * LANES])], axis=1))
            o_s[b, u["rws"], u["sl"]] = po[:, 0:LANES]
            qa_s[b, u["c"], u["p"], 0:CHUNK, :] = (u["qd"] - po[:, LANES:2 * LANES]).astype(BF16)
        return carry

    lax.fori_loop(0, nb * groups_per_batch, solve_group, 0)

    def scan_chunk(c, carry):
        rws = pl.ds(pl.multiple_of(c * CHUNK, CHUNK), CHUNK)
        chains = [(b, p) for b in range(nb) for p in range(N_PAIRS)]
        states = [s_ref[b, p] for b, p in chains]
        prods = [_dot(qa_s[b, c, p], blockdiag(s.astype(BF16))) for (b, p), s in zip(chains, states)]
        for (b, p), s, r in zip(chains, states, prods):
            sl = slice(p * LANES, (p + 1) * LANES)
            s_ref[b, p] = gt_s[b, c, p:p + 1, :] * s + r[CHUNK:2 * CHUNK] + bb_s[b, c, p]
            o_s[b, rws, sl] = o_s[b, rws, sl] + r[0:CHUNK]
        return carry

    lax.fori_loop(0, nc, scan_chunk, 0)

    slabs = [(b, slice(p * LANES, (p + 1) * LANES)) for b in range(nb) for p in range(N_PAIRS)]
    sums = [_group_sumsq(o_s[b, :, sl], ones2) for b, sl in slabs]
    for (b, sl), ss in zip(slabs, sums):
        o_ref[b, :, sl] = (o_s[b, :, sl] * lax.rsqrt(ss * (1.0 / 64.0) + EPS) * ng_ref[...]
                           * _silu(z_ref[b, :, sl])).astype(o_ref.dtype)


def _chunk_triangles(tb):
    r = np.arange(tb)[:, None]
    c = np.arange(tb)[None, :]
    same = (r // CHUNK) == (c // CHUNK)
    c_sw = np.where((c % LANES) < CHUNK, c + CHUNK, c - CHUNK)
    tril = same & (c <= r)
    triu = same & (r <= c)
    triu_sw = ((r // CHUNK) == (c_sw // CHUNK)) & (r <= c_sw)
    return jnp.asarray(np.stack([tril, triu, triu_sw]).astype(np.float32), BF16)


def _gdn(h3, arow, dtrow, ebg, ng, l):
    batch, seq, _ = h3.shape
    tb = min(GDN_TB, seq)
    nc = tb // CHUNK
    hspec = lambda width, col: pl.BlockSpec((batch, tb, width), lambda t: (0, t, col))
    return pl.pallas_call(
        _gdn_kernel,
        grid=(seq // tb,),
        in_specs=[
            hspec(B_WIDTH, COL_BQ), hspec(B_WIDTH, COL_BK), hspec(B_WIDTH, COL_BV), hspec(B_WIDTH, COL_BZ),
            hspec(GATE_W, COL_G),
            _layer_spec((1, GATE_W), l),
            _layer_spec((1, GATE_W), l),
            pl.BlockSpec((2 * GATE_W, 2 * B_WIDTH), lambda t: (0, 0)),
            pl.BlockSpec((3, tb, tb), lambda t: (0, 0, 0)),
            _layer_spec((1, LANES), l),
        ],
        out_specs=pl.BlockSpec((batch, tb, B_WIDTH), lambda t: (0, t, 0)),
        out_shape=jax.ShapeDtypeStruct((batch, seq, B_WIDTH), BF16),
        scratch_shapes=[
            pltpu.VMEM((batch, tb, 2 * B_WIDTH), F32),
            pltpu.VMEM((batch, nc, 8, LANES), F32),
            pltpu.VMEM((batch, nc, N_PAIRS, 2 * CHUNK, LANES), BF16),
            pltpu.VMEM((batch, nc, N_PAIRS, CHUNK, LANES), F32),
            pltpu.VMEM((batch, nc, 8, LANES), F32),
            pltpu.VMEM((batch, tb, B_WIDTH), F32),
            pltpu.VMEM((batch, N_PAIRS, CHUNK, LANES), F32),
        ],
        compiler_params=pltpu.CompilerParams(dimension_semantics=("arbitrary",), vmem_limit_bytes=VMEM_LIMIT),
        name="gdn",
    )(h3, h3, h3, h3, h3, arow, dtrow, ebg, _chunk_triangles(tb), ng)


def _out_ffn_kernel(l, x_ref, a_ref, b_ref, c_ref, wo_hbm, g2_ref, b2_ref, w1_hbm, w3_hbm, w2_hbm, g3_ref, b3_ref,
                    o_ref, act_ref, w1_ref, w3_ref, w2_ref, col_stage, row_stage, wo_ref, sem):
    @pl.when(pl.program_id(0) == 0)
    def _():
        plan = _ffn_weight_plan(l, w1_hbm, w3_hbm, w2_hbm, w1_ref, w3_ref, w2_ref, col_stage, row_stage)
        plan += [(wo_hbm.at[l, pl.ds(j * FFN_TF, FFN_TF), :], row_stage, 1,
                  _store_cast(wo_ref, slice(j * FFN_TF, (j + 1) * FFN_TF), slice(None)))
                 for j in range(D_MODEL // FFN_TF)]
        _stream_in(plan, sem)

    slabs = _row_slabs(x_ref.shape[0])
    mixes = []
    for rows in slabs:
        mix = _dot(a_ref[rows, :], wo_ref[0:A_WIDTH, :])
        mix = mix + _dot(b_ref[rows, :], wo_ref[A_WIDTH:A_WIDTH + B_WIDTH, :])
        mixes.append(mix + _dot(c_ref[rows, :], wo_ref[A_WIDTH + B_WIDTH:, :]))
    for rows, mix in zip(slabs, mixes):
        x2 = _layer_norm(ALPHA * x_ref[rows, :] + mix, g2_ref[...], b2_ref[...])
        o_ref[rows, :] = _ffn_ln_rows(x2, rows, w1_ref, w3_ref, w2_ref, g3_ref, b3_ref, act_ref)


def _out_ffn(x, a, b, c, wo, g2, b2, w1, w3, w2, g3, b3, l):
    n = x.shape[0]
    tm = min(OUT_FFN_TM, n)
    rows = lambda width: pl.BlockSpec((tm, width), lambda i: (i, 0))
    return pl.pallas_call(
        functools.partial(_out_ffn_kernel, l),
        grid=(n // tm,),
        in_specs=[
            rows(D_MODEL), rows(A_WIDTH), rows(B_WIDTH), rows(C_WIDTH),
            HBM,
            _layer_spec((1, D_MODEL), l),
            _layer_spec((1, D_MODEL), l),
            HBM, HBM, HBM,
            _layer_spec((1, D_MODEL), l),
            _layer_spec((1, D_MODEL), l),
        ],
        out_specs=rows(D_MODEL),
        out_shape=jax.ShapeDtypeStruct((n, D_MODEL), F32),
        scratch_shapes=_ffn_scratch(tm) + [pltpu.VMEM((D_MODEL, D_MODEL), BF16),
                                           pltpu.SemaphoreType.DMA((2, STREAM_DEPTH))],
        compiler_params=pltpu.CompilerParams(dimension_semantics=("arbitrary",), vmem_limit_bytes=VMEM_LIMIT),
        name="out_ffn",
    )(x, a, b, c, wo, g2, b2, w1, w3, w2, g3, b3)


def _bcast_selector():
    e = np.zeros((2 * GATE_W, 2 * B_WIDTH), np.float32)
    for j in range(B_WIDTH):
        hd = j // 64
        for base in (0, GATE_W):
            e[base + hd, j] = 1.0
            e[base + B_HEADS + hd, B_WIDTH + j] = 1.0
    return jnp.asarray(e, BF16)


def _gate_row(vec, offset):
    l, n = vec.shape
    return jnp.zeros((l, 1, GATE_W), F32).at[:, 0, offset:offset + n].set(vec)


def kernel(x, ffn1_w1, ffn1_w3, ffn1_w2, ln1_g, ln1_b, w_in, sgu_ln_g, sgu_ln_b, sgu_w, sgu_b, gdn_conv_w, gdn_a_log, gdn_dt_bias, gdn_norm_g, gla_gate_up, gla_gate_b, gla_norm_g, w_out, ln2_g, ln2_b, ffn2_w1, ffn2_w3, ffn2_w2, ln3_g, ln3_b):
    batch, seq, d = x.shape
    depth = w_in.shape[0]
    row = lambda p: p[:, None, :]
    f1 = (ffn1_w1, ffn1_w3, ffn1_w2)
    f2 = (ffn2_w1, ffn2_w3, ffn2_w2)
    w_in_b = w_in.astype(BF16)
    sgu_bias = jnp.repeat(jnp.swapaxes(sgu_b, 1, 2), HEAD_DIM, axis=2)
    arow = _gate_row(gdn_a_log, B_HEADS)
    dtrow = _gate_row(gdn_dt_bias, B_HEADS)
    ebg = _bcast_selector()
    gdn_ng = jnp.tile(gdn_norm_g, (1, 2))[:, None, :]
    gla_ng = jnp.tile(gla_norm_g, (1, C_HEADS))[:, None, :]
    gup = jnp.zeros((depth, GATE_W, C_KEY), F32).at[:, 2 * B_HEADS:2 * B_HEADS + C_RANK, :].set(gla_gate_up)

    xf = x.reshape(batch * seq, d)
    for l in range(depth):
        xf = _ffn_ln(xf, *f1, row(ln1_g), row(ln1_b), l)
        h, out_a = _proj_sgu(xf, w_in_b, row(sgu_ln_g), row(sgu_ln_b), sgu_w, sgu_bias, gdn_conv_w, seq, l)
        h3 = h.reshape(batch, seq, D_H)
        out_b = _gdn(h3, arow, dtrow, ebg, gdn_ng, l).reshape(batch * seq, B_WIDTH)
        out_c = _gla(h3, gup, row(gla_gate_b), gla_ng, l).reshape(batch * seq, C_WIDTH)
        xf = _out_ffn(xf, out_a, out_b, out_c, w_out, row(ln2_g), row(ln2_b), *f2, row(ln3_g), row(ln3_b), l)
    return xf.reshape(batch, seq, d)
```

```python
import functools

import numpy as np
import jax
import jax.numpy as jnp
from jax import lax
from jax.experimental import pallas as pl
from jax.experimental.pallas import tpu as pltpu

F32 = jnp.float32
BF16 = jnp.bfloat16

D_MODEL = 1024
DEPTH = 4
HEAD_DIM = 64
A_HEADS = 4
A_WIDTH = A_HEADS * HEAD_DIM
A_CHUNK = 128
B_HEADS = 8
B_DK = 64
B_WIDTH = B_HEADS * 64
B_CONV = 4
C_HEADS = 4
C_DK = 32
C_DV = 64
C_KEY = C_HEADS * C_DK
C_WIDTH = C_HEADS * C_DV
C_RANK = 16
C_TAU = 16.0
D_FF = 2816
ALPHA = (2.0 * DEPTH) ** 0.25
EPS = 1e-5

LANES = 128
CHUNK = 64
GATE_W = LANES
N_PAIRS = B_HEADS // 2
TAIL = 8
D_IN_PAD = 2 * A_WIDTH + 4 * B_WIDTH + 2 * C_WIDTH + 2 * C_KEY + GATE_W
VMEM_LIMIT = 56 * 1024 * 1024

D_H = D_IN_PAD - 2 * A_WIDTH

COL_BQ, COL_BK, COL_BV, COL_BZ = 0, 1, 2, 3
COL_CV, COL_CR = 8, 9
COL_CQ, COL_CK, COL_G = 20, 21, 22


def _dot(a, b, **kw):
    return jnp.dot(a, b, preferred_element_type=F32, **kw)


def _dot_nt(a, b, **kw):
    return lax.dot_general(a, b, (((1,), (1,)), ((), ())), preferred_element_type=F32, **kw)


def _dot_tn(a, b, **kw):
    return lax.dot_general(a, b, (((0,), (0,)), ((), ())), preferred_element_type=F32, **kw)


def _split2(x):
    hi = x.astype(BF16)
    lo = (x - hi.astype(F32)).astype(BF16)
    return jnp.concatenate([hi, lo], axis=-1)


def _split3(x, axis):
    hi = x.astype(BF16)
    r1 = x - hi.astype(F32)
    mid = r1.astype(BF16)
    lo = (r1 - mid.astype(F32)).astype(BF16)
    return jnp.concatenate([hi, mid, lo], axis=axis)


def _iota(shape, dim):
    return lax.broadcasted_iota(jnp.int32, shape, dim)


def _group_ones(width, group):
    r = _iota((width, width), 0)
    c = _iota((width, width), 1)
    return jnp.where((r // group) == (c // group), 1.0, 0.0).astype(BF16)


def _group_sumsq(x, ones):
    return _dot((x * x).astype(BF16), ones)


def _layer_norm(y, g, b):
    mu = jnp.mean(y, axis=-1, keepdims=True)
    yc = y - mu
    var = jnp.mean(yc * yc, axis=-1, keepdims=True)
    return yc * lax.rsqrt(var + EPS) * g + b


def _silu(x):
    return x * jax.nn.sigmoid(x)


def _gelu_tanh(x):
    return 0.5 * x * (1.0 + jnp.tanh(0.7978845608028654 * (x + 0.044715 * (x * x * x))))


def _softplus(x):
    return jnp.maximum(x, 0.0) + jnp.log(1.0 + jnp.exp(-jnp.abs(x)))


FFN_TM = 1024
OUT_FFN_TM = 1024
FFN_TF = 256
FFN_SLAB_ROWS = 256
STREAM_DEPTH = 4


N_FF_CHUNKS = D_FF // FFN_TF


def _row_slabs(tm):
    return [slice(r, r + FFN_SLAB_ROWS) for r in range(0, tm, FFN_SLAB_ROWS)]


def _ffn_ln_rows(x, rows, w1b, w3b, w2b, g_ref, b_ref, a_ref):
    xb = x.astype(BF16)
    for j in range(N_FF_CHUNKS):
        sl = slice(j * FFN_TF, (j + 1) * FFN_TF)
        h1 = _dot(xb, w1b[:, sl])
        h3 = _dot(xb, w3b[:, sl])
        a_ref[rows, sl] = (_silu(h1) * h3).astype(BF16)
    y = ALPHA * x + 0.5 * _dot(a_ref[rows, :], w2b[...])
    return _layer_norm(y, g_ref[...], b_ref[...])


def _stream_in(plan, sem):
    used = {}
    steps = []
    for src, stage, sid, consume in plan:
        slot = used.get(sid, 0) % STREAM_DEPTH
        used[sid] = used.get(sid, 0) + 1
        steps.append((pltpu.make_async_copy(src, stage.at[slot], sem.at[sid, slot]), stage, slot, consume))
    ahead = STREAM_DEPTH - 1
    for k in range(min(ahead, len(steps))):
        steps[k][0].start()
    for k, (copy, stage, slot, consume) in enumerate(steps):
        if k + ahead < len(steps):
            steps[k + ahead][0].start()
        copy.wait()
        consume(stage[slot])


def _store_cast(dst, rows, cols):
    def consume(v):
        dst[rows, cols] = v.astype(BF16)
    return consume


def _ffn_weight_plan(l, w1_hbm, w3_hbm, w2_hbm, w1b, w3b, w2b, col_stage, row_stage):
    plan = []
    full = slice(None)
    for j in range(N_FF_CHUNKS):
        win = pl.ds(j * FFN_TF, FFN_TF)
        sl = slice(j * FFN_TF, (j + 1) * FFN_TF)
        plan.append((w1_hbm.at[l, :, win], col_stage, 0, _store_cast(w1b, full, sl)))
        plan.append((w3_hbm.at[l, :, win], col_stage, 0, _store_cast(w3b, full, sl)))
        plan.append((w2_hbm.at[l, win, :], row_stage, 1, _store_cast(w2b, sl, full)))
    return plan


def _ffn_scratch(tm):
    return [
        pltpu.VMEM((tm, D_FF), BF16),
        pltpu.VMEM((D_MODEL, D_FF), BF16),
        pltpu.VMEM((D_MODEL, D_FF), BF16),
        pltpu.VMEM((D_FF, D_MODEL), BF16),
        pltpu.VMEM((STREAM_DEPTH, D_MODEL, FFN_TF), F32),
        pltpu.VMEM((STREAM_DEPTH, FFN_TF, D_MODEL), F32),
    ]


def _ffn_ln_kernel(l, x_ref, w1_hbm, w3_hbm, w2_hbm, g_ref, b_ref, o_ref,
                   a_ref, w1b, w3b, w2b, col_stage, row_stage, sem):
    @pl.when(pl.program_id(0) == 0)
    def _():
        _stream_in(_ffn_weight_plan(l, w1_hbm, w3_hbm, w2_hbm, w1b, w3b, w2b, col_stage, row_stage), sem)

    for rows in _row_slabs(x_ref.shape[0]):
        o_ref[rows, :] = _ffn_ln_rows(x_ref[rows, :], rows, w1b, w3b, w2b, g_ref, b_ref, a_ref)


def _layer_spec(shape, l, **kw):
    zeros = (0,) * len(shape)
    return pl.BlockSpec((None,) + tuple(shape), lambda *_: (l,) + zeros, **kw)


HBM = pl.BlockSpec(memory_space=pl.ANY)


def _ffn_ln(x, w1, w3, w2, g, b, l):
    n = x.shape[0]
    tm = min(FFN_TM, n)
    return pl.pallas_call(
        functools.partial(_ffn_ln_kernel, l),
        grid=(n // tm,),
        in_specs=[
            pl.BlockSpec((tm, D_MODEL), lambda i: (i, 0)),
            HBM, HBM, HBM,
            _layer_spec((1, D_MODEL), l),
            _layer_spec((1, D_MODEL), l),
        ],
        out_specs=pl.BlockSpec((tm, D_MODEL), lambda i: (i, 0)),
        out_shape=jax.ShapeDtypeStruct((n, D_MODEL), F32),
        scratch_shapes=_ffn_scratch(tm) + [pltpu.SemaphoreType.DMA((2, STREAM_DEPTH))],
        compiler_params=pltpu.CompilerParams(dimension_semantics=("arbitrary",), vmem_limit_bytes=VMEM_LIMIT),
        name="ffn_ln",
    )(x, w1, w3, w2, g, b)


PROJ_TM = 512
PROJ_WR = 128
D_IN = 2 * A_WIDTH + 4 * B_WIDTH + 2 * B_HEADS + 2 * C_KEY + 2 * C_WIDTH + C_RANK


def _permute_w_in(v):
    o = np.cumsum([0, A_WIDTH, A_WIDTH, 3 * B_WIDTH, B_WIDTH, B_HEADS, B_HEADS, C_KEY, C_KEY, C_WIDTH, C_WIDTH, C_RANK])
    front = v[:, 0:o[4]]
    gates_b = v[:, o[4]:o[6]]
    c_qk = v[:, o[6]:o[8]]
    c_vr = v[:, o[8]:o[10]]
    c_g = v[:, o[10]:o[11]]
    pad = jnp.zeros((v.shape[0], GATE_W - 2 * B_HEADS - C_RANK), F32)
    return jnp.concatenate([front, c_vr, c_qk, gates_b, c_g, pad], axis=-1).astype(BF16)


def _proj_sgu_kernel(tiles_per_seq, l, x_ref, w_hbm, lng_ref, lnb_ref, sw_ref, bias_ref, cw_ref, h_ref, a_ref,
                     xbuf, w_ref, w_stage, sem):
    tm = x_ref.shape[0]
    w3 = 3 * B_WIDTH
    c0 = 2 * A_WIDTH

    @pl.when(pl.program_id(0) == 0)
    def _():
        def store_rows(r):
            def consume(v):
                w_ref[r * PROJ_WR:(r + 1) * PROJ_WR, :] = _permute_w_in(v.astype(F32))
            return consume
        _stream_in([(w_hbm.at[l, pl.ds(r * PROJ_WR, PROJ_WR), :], w_stage, 0, store_rows(r))
                    for r in range(D_MODEL // PROJ_WR)], sem)

    @pl.when(pl.program_id(0) % tiles_per_seq == 0)
    def _():
        xbuf[0:TAIL, :] = jnp.zeros((TAIL, w3), F32)

    xb = x_ref[...].astype(BF16)

    def conv_silu(part):
        cols = slice(part * B_WIDTH, (part + 1) * B_WIDTH)
        acc = cw_ref[B_CONV - 1:B_CONV, cols] * xbuf[TAIL:TAIL + tm, cols]
        for i in range(B_CONV - 1):
            back = B_CONV - 1 - i
            acc = acc + cw_ref[i:i + 1, cols] * xbuf[TAIL - back:TAIL - back + tm, cols]
        xbuf[0:TAIL, cols] = xbuf[tm:tm + TAIL, cols]
        return _silu(acc)

    def project(part):
        cols = slice(part * B_WIDTH, (part + 1) * B_WIDTH)
        xbuf[TAIL:TAIL + tm, cols] = _dot(xb, w_ref[:, c0 + part * B_WIDTH:c0 + (part + 1) * B_WIDTH])

    ones2 = _group_ones(LANES, B_DK)

    def l2_normalise(y, part, scale):
        for p in range(N_PAIRS):
            yp = y[:, p * LANES:(p + 1) * LANES]
            h_ref[:, part * B_WIDTH + p * LANES:part * B_WIDTH + (p + 1) * LANES] = (
                yp * (lax.rsqrt(_group_sumsq(yp, ones2) + 1e-6) * scale))

    ha = _dot(xb, w_ref[:, 0:c0])
    project(0)
    u = _gelu_tanh(ha[:, 0:A_WIDTH])
    v = _layer_norm(_gelu_tanh(ha[:, A_WIDTH:c0]), lng_ref[...], lnb_ref[...])
    project(1)

    tril = _iota((A_CHUNK, A_CHUNK), 1) <= _iota((A_CHUNK, A_CHUNK), 0)
    head_of_lane = _iota((A_CHUNK, A_WIDTH), 1) // HEAD_DIM
    wts = [jnp.where(tril, sw_ref[h], 0.0).astype(BF16) for h in range(A_HEADS)]
    bias = bias_ref[...]
    for c in range(tm // A_CHUNK):
        rows = slice(c * A_CHUNK, (c + 1) * A_CHUNK)
        vc = v[rows]
        z = bias
        for h in range(A_HEADS):
            z = z + _dot(wts[h], jnp.where(head_of_lane == h, vc, 0.0).astype(BF16))
        a_ref[rows, :] = (u[rows] * z).astype(a_ref.dtype)

    yq = conv_silu(0)
    project(2)
    yk = conv_silu(1)
    l2_normalise(yq, 0, B_DK ** -0.5)
    h_ref[:, w3:] = _dot(xb, w_ref[:, c0 + w3:])
    h_ref[:, 2 * B_WIDTH:w3] = conv_silu(2)
    l2_normalise(yk, 1, 1.0)


def _proj_sgu(x, w, lng, lnb, sw, bias, cw, seq, l):
    n = x.shape[0]
    tm = min(PROJ_TM, seq)
    return pl.pallas_call(
        functools.partial(_proj_sgu_kernel, seq // tm, l),
        grid=(n // tm,),
        in_specs=[
            pl.BlockSpec((tm, D_MODEL), lambda i: (i, 0)),
            HBM,
            _layer_spec((1, A_WIDTH), l),
            _layer_spec((1, A_WIDTH), l),
            _layer_spec((A_HEADS, A_CHUNK, A_CHUNK), l),
            _layer_spec((A_CHUNK, A_WIDTH), l),
            _layer_spec((B_CONV, 3 * B_WIDTH), l),
        ],
        out_specs=[pl.BlockSpec((tm, D_H), lambda i: (i, 0)), pl.BlockSpec((tm, A_WIDTH), lambda i: (i, 0))],
        out_shape=[jax.ShapeDtypeStruct((n, D_H), F32), jax.ShapeDtypeStruct((n, A_WIDTH), BF16)],
        scratch_shapes=[
            pltpu.VMEM((tm + TAIL, 3 * B_WIDTH), F32),
            pltpu.VMEM((D_MODEL, D_IN_PAD), BF16),
            pltpu.VMEM((STREAM_DEPTH, PROJ_WR, D_IN), BF16),
            pltpu.SemaphoreType.DMA((1, STREAM_DEPTH)),
        ],
        compiler_params=pltpu.CompilerParams(dimension_semantics=("arbitrary",), vmem_limit_bytes=VMEM_LIMIT),
        name="proj_sgu",
    )(x, w, lng, lnb, sw, bias, cw)


GLA_TB = 512
GLA_TRI = 256


def _gla_kernel(q_ref, k_ref, v_ref, r_ref, g_ref, gup_ref, gb_ref, tri_ref, ng_ref, o_ref, st_ref, ob_ref):
    nb, tb = q_ref.shape[0], q_ref.shape[1]
    nc = tb // CHUNK

    @pl.when(pl.program_id(0) == 0)
    def _():
        st_ref[...] = jnp.zeros_like(st_ref)

    khead = _iota((CHUNK, C_KEY), 1) // C_DK
    vhead = _iota((CHUNK, C_WIDTH), 1) // C_DV
    causal = (_iota((CHUNK, C_WIDTH), 1) % CHUNK) <= _iota((CHUNK, C_WIDTH), 0)
    st_mask = (_iota((C_WIDTH, C_KEY), 0) // C_DV) == (_iota((C_WIDTH, C_KEY), 1) // C_DK)

    units = []
    gup = gup_ref[...]
    u_hi = gup.astype(BF16)
    u_lo = (gup - u_hi.astype(F32)).astype(BF16)
    gup3 = jnp.concatenate([u_hi, u_hi, u_lo], axis=0)
    tri_rows = tri_ref.shape[0]
    for b in range(nb):
        g = g_ref[b]
        g_hi = g.astype(BF16)
        g_lo = (g - g_hi.astype(F32)).astype(BF16)
        logit = _dot(jnp.concatenate([g_hi, g_lo, g_hi], axis=1), gup3) + gb_ref[...]
        log_a = -_softplus(-logit) * (1.0 / C_TAU)
        bc_parts = []
        for r in range(tb // tri_rows):
            g3 = _dot(tri_ref[...], _split3(log_a[r * tri_rows:(r + 1) * tri_rows], 1))
            bc_parts.append(g3[:, 0:C_KEY] + g3[:, C_KEY:2 * C_KEY] + g3[:, 2 * C_KEY:3 * C_KEY])
        for c in range(nc):
            rows = slice(c * CHUNK, (c + 1) * CHUNK)
            part, off = divmod(c * CHUNK, tri_rows)
            bc = bc_parts[part][off:off + CHUNK]
            btot = bc[CHUNK - 1:CHUNK, :]
            k = k_ref[b, rows, :]
            kn = k * jnp.exp(-bc)
            vb = v_ref[b, rows, :].astype(BF16)
            units.append(dict(
                b=b, rows=rows, vb=vb, gt=jnp.exp(btot),
                qd=(q_ref[b, rows, :] * ((C_DK ** -0.5) * jnp.exp(bc))).astype(BF16),
                kd=(k * jnp.exp(btot - bc)).astype(BF16),
                kstack=jnp.concatenate([jnp.where(khead == h, kn, 0.0) for h in range(C_HEADS)], axis=0).astype(BF16),
                vbd=jnp.concatenate([jnp.where(vhead == h, vb, jnp.zeros_like(vb)) for h in range(C_HEADS)], axis=0)))
    for u in units:
        u["p"] = jnp.where(causal, _dot_nt(u["qd"], u["kstack"]), 0.0).astype(BF16)
    for u in units:
        u["upd"] = jnp.where(st_mask, _dot_tn(u["vb"], u["kd"]), 0.0)
    for u in units:
        u["oi"] = _dot(u["p"], u["vbd"])

    states = [st_ref[b] for b in range(nb)]
    for u in units:
        b = u["b"]
        ob_ref[b, u["rows"], :] = u["oi"] + _dot_nt(u["qd"], states[b].astype(BF16))
        states[b] = u["gt"] * states[b] + u["upd"]
    for b in range(nb):
        st_ref[b] = states[b]

    ones = _group_ones(C_WIDTH, C_DV)
    for b in range(nb):
        o = ob_ref[b]
        ss = _group_sumsq(o, ones)
        o_ref[b] = (o * lax.rsqrt(ss * (1.0 / C_DV) + EPS) * ng_ref[...] * _silu(r_ref[b])).astype(o_ref.dtype)


def _gla(h3, gup, gb, ng, l):
    batch, seq, _ = h3.shape
    tb = min(GLA_TB, seq)
    hspec = lambda width, col: pl.BlockSpec((batch, tb, width), lambda t: (0, t, col))
    return pl.pallas_call(
        _gla_kernel,
        grid=(seq // tb,),
        in_specs=[
            hspec(C_KEY, COL_CQ), hspec(C_KEY, COL_CK), hspec(C_WIDTH, COL_CV), hspec(C_WIDTH, COL_CR),
            hspec(GATE_W, COL_G),
            _layer_spec((GATE_W, C_KEY), l),
            _layer_spec((1, C_KEY), l),
            pl.BlockSpec((GLA_TRI, GLA_TRI), lambda t: (0, 0)),
            _layer_spec((1, C_WIDTH), l),
        ],
        out_specs=pl.BlockSpec((batch, tb, C_WIDTH), lambda t: (0, t, 0)),
        out_shape=jax.ShapeDtypeStruct((batch, seq, C_WIDTH), BF16),
        scratch_shapes=[pltpu.VMEM((batch, C_WIDTH, C_KEY), F32), pltpu.VMEM((batch, tb, C_WIDTH), F32)],
        compiler_params=pltpu.CompilerParams(dimension_semantics=("arbitrary",), vmem_limit_bytes=VMEM_LIMIT),
        name="gla",
    )(h3, h3, h3, h3, h3, gup, gb, _chunk_triangles(GLA_TRI)[0], ng)


GDN_TB = 256
GROUP_CHUNKS = 4


def _gdn_kernel(q_ref, k_ref, v_ref, z_ref, g_ref, arow_ref, dtrow_ref, ebg_ref, tri_ref, ng_ref, o_ref,
                bg_s, rows_s, qa_s, bb_s, gt_s, o_s, s_ref):
    nb, tb = q_ref.shape[0], q_ref.shape[1]
    nc = tb // CHUNK

    @pl.when(pl.program_id(0) == 0)
    def _():
        s_ref[...] = jnp.zeros_like(s_ref)

    ones2 = _group_ones(LANES, B_DK)
    lane = _iota((tb, GATE_W), 1)
    low_half = _iota((1, LANES), 1) < CHUNK
    batches = range(nb)
    nh2 = 2 * B_HEADS
    sum3 = lambda t, n, ax: (lax.slice_in_dim(t, 0, n, axis=ax) + lax.slice_in_dim(t, n, 2 * n, axis=ax)
                             + lax.slice_in_dim(t, 2 * n, 3 * n, axis=ax))
    betas = [jax.nn.sigmoid(g_ref[b]) for b in batches]
    lgs = [jnp.where((lane >= B_HEADS) & (lane < nh2),
                     -jnp.exp(arow_ref[...]) * _softplus(g_ref[b] + dtrow_ref[...]), 0.0) for b in batches]
    gams = [sum3(_dot(tri_ref[0], _split3(lg, 1)), GATE_W, 1) for lg in lgs]
    lg_t3s = [_split3(lg.T[0:nh2, :], 0) for lg in lgs]
    gam_ts = [sum3(_dot(t, tri_ref[1]), nh2, 0) for t in lg_t3s]
    gam_t_sws = [sum3(_dot(t, tri_ref[2]), nh2, 0) for t in lg_t3s]
    for b in batches:
        feat = jnp.where(lane < B_HEADS, betas[b], gams[b])
        bg_s[b] = _dot(_split2(feat), ebg_ref[...])
    for b in batches:
        for c in range(nc):
            tile = slice((c // 2) * LANES, (c // 2 + 1) * LANES)
            first, second = (gam_ts[b], gam_t_sws[b]) if c % 2 == 0 else (gam_t_sws[b], gam_ts[b])
            for p in range(N_PAIRS):
                h0 = B_HEADS + 2 * p
                rows_s[b, c, p:p + 1, :] = jnp.where(low_half, first[h0:h0 + 1, tile], second[h0 + 1:h0 + 2, tile])

    ii = _iota((CHUNK, LANES), 0)
    jj = _iota((CHUNK, LANES), 1) % CHUNK
    incl = jj <= ii
    strict = jj < ii
    eye = jnp.where(jj == ii, 1.0, 0.0)
    lo_lane = _iota((CHUNK, LANES), 1) < CHUNK

    def blockdiag(m):
        zero = jnp.zeros_like(m)
        return jnp.concatenate([jnp.where(lo_lane, m, zero), jnp.where(lo_lane, zero, m)], axis=0)

    groups_per_batch = nc // GROUP_CHUNKS

    def solve_group(g, carry):
        b = g // groups_per_batch
        c0 = (g % groups_per_batch) * GROUP_CHUNKS
        units = []
        for dc in range(GROUP_CHUNKS):
            c = c0 + dc
            rws = pl.ds(pl.multiple_of(c * CHUNK, CHUNK), CHUNK)
            grows = rows_s[b, c]
            for p in range(N_PAIRS):
                sl = slice(p * LANES, (p + 1) * LANES)
                q = q_ref[b, rws, sl]
                k = k_ref[b, rws, sl]
                v = v_ref[b, rws, sl]
                bet = bg_s[b, rws, sl]
                gi = bg_s[b, rws, B_WIDTH + p * LANES:B_WIDTH + (p + 1) * LANES]
                dec = jnp.exp(jnp.minimum(gi - grows[p:p + 1, :], 0.0))
                eg = jnp.exp(gi)
                glast = gi[CHUNK - 1:CHUNK, :]
                k16 = k.astype(BF16)
                qk = _dot_nt(jnp.concatenate([q.astype(BF16), k16], axis=0), blockdiag(k16))
                nk = jnp.where(strict, qk[CHUNK:2 * CHUNK] * (-bet) * dec, 0.0)
                gt_s[b, c, p:p + 1, :] = jnp.exp(glast)
                units.append(dict(
                    c=c, p=p, rws=rws, sl=sl,
                    pm=jnp.where(incl, qk[0:CHUNK] * dec, 0.0).astype(BF16),
                    tm=eye + nk, nk=nk.astype(BF16),
                    rhs=jnp.concatenate([blockdiag((v * bet).astype(BF16)),
                                         blockdiag((k * (bet * eg)).astype(BF16))], axis=1),
                    qd=q * eg, kd=(k * jnp.exp(glast - gi)).astype(BF16)))
        for u in units:
            u["nk"] = _dot(u["nk"], blockdiag(u["nk"])).astype(BF16)
        for _ in range(4):
            for u in units:
                both = _dot(u["nk"], jnp.concatenate([blockdiag(u["nk"]), blockdiag(u["tm"].astype(BF16))], axis=1))
                u["nk"] = both[:, 0:LANES].astype(BF16)
                u["tm"] = u["tm"] + both[:, LANES:2 * LANES]
        for u in units:
            u["tm"] = u["tm"] + _dot(u["nk"], blockdiag(u["tm"].astype(BF16)))
        for u in units:
            u["sol"] = _dot(u["tm"].astype(BF16), u["rhs"]).astype(BF16)
        for u in units:
            ab = _dot_tn(u["kd"], u["sol"])
            bb_s[b, u["c"], u["p"]] = jnp.where(lo_lane, ab[0:CHUNK, 0:LANES], ab[CHUNK:2 * CHUNK, 0:LANES])
            qa_s[b, u["c"], u["p"], CHUNK:2 * CHUNK, :] = (
                -jnp.where(lo_lane, ab[0:CHUNK, LANES:2 * LANES], ab[CHUNK:2 * CHUNK, LANES:2 * LANES])).astype(BF16)
        for u in units:
            sol = u["sol"]
            po = _dot(u["pm"], jnp.concatenate([blockdiag(sol[:, 0:LANES]), blockdiag(sol[:, LANES:2 * LANES])], axis=1))
            o_s[b, u["rws"], u["sl"]] = po[:, 0:LANES]
            qa_s[b, u["c"], u["p"], 0:CHUNK, :] = (u["qd"] - po[:, LANES:2 * LANES]).astype(BF16)
        return carry

    lax.fori_loop(0, nb * groups_per_batch, solve_group, 0)

    def scan_chunk(c, carry):
        rws = pl.ds(pl.multiple_of(c * CHUNK, CHUNK), CHUNK)
        chains = [(b, p) for b in range(nb) for p in range(N_PAIRS)]
        states = [s_ref[b, p] for b, p in chains]
        prods = [_dot(qa_s[b, c, p], blockdiag(s.astype(BF16))) for (b, p), s in zip(chains, states)]
        for (b, p), s, r in zip(chains, states, prods):
            sl = slice(p * LANES, (p + 1) * LANES)
            s_ref[b, p] = gt_s[b, c, p:p + 1, :] * s + r[CHUNK:2 * CHUNK] + bb_s[b, c, p]
            o_s[b, rws, sl] = o_s[b, rws, sl] + r[0:CHUNK]
        return carry

    lax.fori_loop(0, nc, scan_chunk, 0)

    slabs = [(b, slice(p * LANES, (p + 1) * LANES)) for b in range(nb) for p in range(N_PAIRS)]
    sums = [_group_sumsq(o_s[b, :, sl], ones2) for b, sl in slabs]
    for (b, sl), ss in zip(slabs, sums):
        o_ref[b, :, sl] = (o_s[b, :, sl] * lax.rsqrt(ss * (1.0 / 64.0) + EPS) * ng_ref[...]
                           * _silu(z_ref[b, :, sl])).astype(o_ref.dtype)


def _chunk_triangles(tb):
    r = np.arange(tb)[:, None]
    c = np.arange(tb)[None, :]
    same = (r // CHUNK) == (c // CHUNK)
    c_sw = np.where((c % LANES) < CHUNK, c + CHUNK, c - CHUNK)
    tril = same & (c <= r)
    triu = same & (r <= c)
    triu_sw = ((r // CHUNK) == (c_sw // CHUNK)) & (r <= c_sw)
    return jnp.asarray(np.stack([tril, triu, triu_sw]).astype(np.float32), BF16)


def _gdn(h3, arow, dtrow, ebg, ng, l):
    batch, seq, _ = h3.shape
    tb = min(GDN_TB, seq)
    nc = tb // CHUNK
    hspec = lambda width, col: pl.BlockSpec((batch, tb, width), lambda t: (0, t, col))
    return pl.pallas_call(
        _gdn_kernel,
        grid=(seq // tb,),
        in_specs=[
            hspec(B_WIDTH, COL_BQ), hspec(B_WIDTH, COL_BK), hspec(B_WIDTH, COL_BV), hspec(B_WIDTH, COL_BZ),
            hspec(GATE_W, COL_G),
            _layer_spec((1, GATE_W), l),
            _layer_spec((1, GATE_W), l),
            pl.BlockSpec((2 * GATE_W, 2 * B_WIDTH), lambda t: (0, 0)),
            pl.BlockSpec((3, tb, tb), lambda t: (0, 0, 0)),
            _layer_spec((1, LANES), l),
        ],
        out_specs=pl.BlockSpec((batch, tb, B_WIDTH), lambda t: (0, t, 0)),
        out_shape=jax.ShapeDtypeStruct((batch, seq, B_WIDTH), BF16),
        scratch_shapes=[
            pltpu.VMEM((batch, tb, 2 * B_WIDTH), F32),
            pltpu.VMEM((batch, nc, 8, LANES), F32),
            pltpu.VMEM((batch, nc, N_PAIRS, 2 * CHUNK, LANES), BF16),
            pltpu.VMEM((batch, nc, N_PAIRS, CHUNK, LANES), F32),
            pltpu.VMEM((batch, nc, 8, LANES), F32),
            pltpu.VMEM((batch, tb, B_WIDTH), F32),
            pltpu.VMEM((batch, N_PAIRS, CHUNK, LANES), F32),
        ],
        compiler_params=pltpu.CompilerParams(dimension_semantics=("arbitrary",), vmem_limit_bytes=VMEM_LIMIT),
        name="gdn",
    )(h3, h3, h3, h3, h3, arow, dtrow, ebg, _chunk_triangles(tb), ng)


def _out_ffn_kernel(l, x_ref, a_ref, b_ref, c_ref, wo_hbm, g2_ref, b2_ref, w1_hbm, w3_hbm, w2_hbm, g3_ref, b3_ref,
                    o_ref, act_ref, w1_ref, w3_ref, w2_ref, col_stage, row_stage, wo_ref, sem):
    @pl.when(pl.program_id(0) == 0)
    def _():
        plan = _ffn_weight_plan(l, w1_hbm, w3_hbm, w2_hbm, w1_ref, w3_ref, w2_ref, col_stage, row_stage)
        plan += [(wo_hbm.at[l, pl.ds(j * FFN_TF, FFN_TF), :], row_stage, 1,
                  _store_cast(wo_ref, slice(j * FFN_TF, (j + 1) * FFN_TF), slice(None)))
                 for j in range(D_MODEL // FFN_TF)]
        _stream_in(plan, sem)

    slabs = _row_slabs(x_ref.shape[0])
    mixes = []
    for rows in slabs:
        mix = _dot(a_ref[rows, :], wo_ref[0:A_WIDTH, :])
        mix = mix + _dot(b_ref[rows, :], wo_ref[A_WIDTH:A_WIDTH + B_WIDTH, :])
        mixes.append(mix + _dot(c_ref[rows, :], wo_ref[A_WIDTH + B_WIDTH:, :]))
    for rows, mix in zip(slabs, mixes):
        x2 = _layer_norm(ALPHA * x_ref[rows, :] + mix, g2_ref[...], b2_ref[...])
        o_ref[rows, :] = _ffn_ln_rows(x2, rows, w1_ref, w3_ref, w2_ref, g3_ref, b3_ref, act_ref)


def _out_ffn(x, a, b, c, wo, g2, b2, w1, w3, w2, g3, b3, l):
    n = x.shape[0]
    tm = min(OUT_FFN_TM, n)
    rows = lambda width: pl.BlockSpec((tm, width), lambda i: (i, 0))
    return pl.pallas_call(
        functools.partial(_out_ffn_kernel, l),
        grid=(n // tm,),
        in_specs=[
            rows(D_MODEL), rows(A_WIDTH), rows(B_WIDTH), rows(C_WIDTH),
            HBM,
            _layer_spec((1, D_MODEL), l),
            _layer_spec((1, D_MODEL), l),
            HBM, HBM, HBM,
            _layer_spec((1, D_MODEL), l),
            _layer_spec((1, D_MODEL), l),
        ],
        out_specs=rows(D_MODEL),
        out_shape=jax.ShapeDtypeStruct((n, D_MODEL), F32),
        scratch_shapes=_ffn_scratch(tm) + [pltpu.VMEM((D_MODEL, D_MODEL), BF16),
                                           pltpu.SemaphoreType.DMA((2, STREAM_DEPTH))],
        compiler_params=pltpu.CompilerParams(dimension_semantics=("arbitrary",), vmem_limit_bytes=VMEM_LIMIT),
        name="out_ffn",
    )(x, a, b, c, wo, g2, b2, w1, w3, w2, g3, b3)


def _bcast_selector():
    e = np.zeros((2 * GATE_W, 2 * B_WIDTH), np.float32)
    for j in range(B_WIDTH):
        hd = j // 64
        for base in (0, GATE_W):
            e[base + hd, j] = 1.0
            e[base + B_HEADS + hd, B_WIDTH + j] = 1.0
    return jnp.asarray(e, BF16)


def _gate_row(vec, offset):
    l, n = vec.shape
    return jnp.zeros((l, 1, GATE_W), F32).at[:, 0, offset:offset + n].set(vec)


def kernel(x, ffn1_w1, ffn1_w3, ffn1_w2, ln1_g, ln1_b, w_in, sgu_ln_g, sgu_ln_b, sgu_w, sgu_b, gdn_conv_w, gdn_a_log, gdn_dt_bias, gdn_norm_g, gla_gate_up, gla_gate_b, gla_norm_g, w_out, ln2_g, ln2_b, ffn2_w1, ffn2_w3, ffn2_w2, ln3_g, ln3_b):
    batch, seq, d = x.shape
    depth = w_in.shape[0]
    row = lambda p: p[:, None, :]
    f1 = (ffn1_w1, ffn1_w3, ffn1_w2)
    f2 = (ffn2_w1, ffn2_w3, ffn2_w2)
    w_in_b = w_in.astype(BF16)
    sgu_bias = jnp.repeat(jnp.swapaxes(sgu_b, 1, 2), HEAD_DIM, axis=2)
    arow = _gate_row(gdn_a_log, B_HEADS)
    dtrow = _gate_row(gdn_dt_bias, B_HEADS)
    ebg = _bcast_selector()
    gdn_ng = jnp.tile(gdn_norm_g, (1, 2))[:, None, :]
    gla_ng = jnp.tile(gla_norm_g, (1, C_HEADS))[:, None, :]
    gup = jnp.zeros((depth, GATE_W, C_KEY), F32).at[:, 2 * B_HEADS:2 * B_HEADS + C_RANK, :].set(gla_gate_up)

    xf = x.reshape(batch * seq, d)
    for l in range(depth):
        xf = _ffn_ln(xf, *f1, row(ln1_g), row(ln1_b), l)
        h, out_a = _proj_sgu(xf, w_in_b, row(sgu_ln_g), row(sgu_ln_b), sgu_w, sgu_bias, gdn_conv_w, seq, l)
        h3 = h.reshape(batch, seq, D_H)
        out_b = _gdn(h3, arow, dtrow, ebg, gdn_ng, l).reshape(batch * seq, B_WIDTH)
        out_c = _gla(h3, gup, row(gla_gate_b), gla_ng, l).reshape(batch * seq, C_WIDTH)
        xf = _out_ffn(xf, out_a, out_b, out_c, w_out, row(ln2_g), row(ln2_b), *f2, row(ln3_g), row(ln3_b), l)
    return xf.reshape(batch, seq, d)
```
